```python
import math
import jax, jax.numpy as jnp
from jax import lax
import numpy as np

D_MODEL = 2048
BATCH = 4
SEQ = 2048
DEPTH = 4
DEC_BATCH = 1
DEC_SEQ = 8192
PAST_LEN = 128

N_EVEN = (DEPTH + 1) // 2
N_ODD = DEPTH // 2
FNET_WIDTH = D_MODEL // 2
FNET_GROUPS = 4
FNET_GROUP_DIM = FNET_WIDTH // FNET_GROUPS
DIFF_HEADS = 8
DIFF_HEAD_DIM = D_MODEL // (4 * DIFF_HEADS)
DIFF_V_DIM = 2 * DIFF_HEAD_DIM
DIFF_QK = DIFF_HEADS * 2 * DIFF_HEAD_DIM
DIFF_WIDTH = DIFF_HEADS * DIFF_V_DIM
EVEN_IN = FNET_WIDTH + 2 * DIFF_QK + DIFF_WIDTH
EVEN_OUT = FNET_WIDTH + DIFF_WIDTH
Q_BLOCK = 128
N_BUCKETS = 32
MAX_DISTANCE = 128
GLA_HEADS = 4
GLA_DK = D_MODEL // 2 // GLA_HEADS
GLA_DV = D_MODEL // GLA_HEADS
GLA_RANK = 16
GLA_TAU = 16.0
GLA_CHUNK = 64
GLA_HK = GLA_HEADS * GLA_DK
GLA_HV = GLA_HEADS * GLA_DV
ODD_IN = 2 * GLA_HK + 2 * GLA_HV
X_HEADS = 4
X_HEAD_DIM = D_MODEL // X_HEADS
N_MEM = 256
D_FF = 5632
EPS = 1e-6

kernel_name = "hybrid_fnet_diffattn_gla_encoder"


def rms_norm(x, g):
    xf = x.astype(jnp.float32)
    y = xf * lax.rsqrt(jnp.mean(xf * xf, axis=-1, keepdims=True) + EPS)
    return (y * g.astype(jnp.float32)).astype(x.dtype)


def t5_bucket(rel):
    n = N_BUCKETS // 2
    max_exact = n // 2
    base = jnp.where(rel > 0, n, 0)
    a = jnp.abs(rel)
    af = jnp.maximum(a, 1).astype(jnp.float32)
    large = max_exact + (jnp.log(af / max_exact) / math.log(MAX_DISTANCE / max_exact)
                         * (n - max_exact)).astype(jnp.int32)
    large = jnp.minimum(large, n - 1)
    return base + jnp.where(a < max_exact, a, large)


def fnet_mix(u):
    B, S, _ = u.shape
    ug = u.reshape(B, S, FNET_GROUPS, FNET_GROUP_DIM).astype(jnp.float32)
    f = jnp.fft.fft2(ug, axes=(1, 3), norm="ortho").real
    return f.reshape(B, S, FNET_WIDTH).astype(u.dtype)


def diff_attention(q, k, v, lam, lam_init, head_g, rel_bias):
    B, S = q.shape[:2]
    nb = S // Q_BLOCK
    qb = (q * (DIFF_HEAD_DIM ** -0.5)).reshape(B, nb, Q_BLOCK, DIFF_HEADS, 2, DIFF_HEAD_DIM)
    qb = qb.transpose(1, 0, 2, 3, 4, 5)
    kpos = jnp.arange(S, dtype=jnp.int32)

    def block(args):
        qi, i = args
        s = jnp.einsum('bqhcd,bkhcd->bhcqk', qi, k).astype(jnp.float32)
        qpos = i * Q_BLOCK + jnp.arange(Q_BLOCK, dtype=jnp.int32)
        bias = rel_bias[t5_bucket(kpos[None, :] - qpos[:, None])]
        s = s + bias.transpose(2, 0, 1)[None, :, None].astype(jnp.float32)
        p = jax.nn.softmax(s, axis=-1)
        a = p[:, :, 0] - lam * p[:, :, 1]
        return jnp.einsum('bhqk,bkhe->bqhe', a.astype(v.dtype), v)

    o = lax.map(block, (qb, jnp.arange(nb, dtype=jnp.int32)))
    o = o.transpose(1, 0, 2, 3, 4).reshape(B, S, DIFF_HEADS, DIFF_V_DIM)
    o = rms_norm(o, head_g) * (1.0 - lam_init)
    return o.reshape(B, S, DIFF_WIDTH)


def even_mixer(h, w_in, w_out, lam_p, head_g, rel_bias, lam_init):
    B, S, _ = h.shape
    z = h @ w_in
    u, q, k, v = jnp.split(z, [FNET_WIDTH, FNET_WIDTH + DIFF_QK, FNET_WIDTH + 2 * DIFF_QK], axis=-1)
    fa = fnet_mix(u)
    q = q.reshape(B, S, DIFF_HEADS, 2, DIFF_HEAD_DIM)
    k = k.reshape(B, S, DIFF_HEADS, 2, DIFF_HEAD_DIM)
    v = v.reshape(B, S, DIFF_HEADS, DIFF_V_DIM)
    lp = lam_p.astype(jnp.float32)
    lam = jnp.exp(jnp.sum(lp[0] * lp[1])) - jnp.exp(jnp.sum(lp[2] * lp[3])) + lam_init
    da = diff_attention(q, k, v, lam, lam_init, head_g, rel_bias)
    return jnp.concatenate([fa, da], axis=-1) @ w_out


def gla_chunked(q, k, v, g):
    B, S, H, DK = q.shape
    DV = v.shape[-1]
    nc = S // GLA_CHUNK

    def to_chunks(t):
        return t.reshape(B, nc, GLA_CHUNK, H, t.shape[-1]).transpose(1, 0, 3, 2, 4)

    lower = jnp.tril(jnp.ones((GLA_CHUNK, GLA_CHUNK), dtype=bool))

    def step(state, inp):
        qc, kc, vc, gc = inp
        b = jnp.cumsum(gc, axis=2)
        inter = jnp.einsum('bhtd,bhde->bhte', qc * jnp.exp(b), state)
        diff = b[:, :, :, None, :] - b[:, :, None, :, :]
        decay = jnp.exp(jnp.where(lower[:, :, None], diff, -jnp.inf))
        attn = jnp.einsum('bhtd,bhsd,bhtsd->bhts', qc, kc, decay)
        intra = jnp.einsum('bhts,bhse->bhte', attn, vc)
        b_last = b[:, :, -1:, :]
        state = (jnp.exp(b_last[:, :, 0, :])[..., None] * state
                 + jnp.einsum('bhsd,bhse->bhde', kc * jnp.exp(b_last - b), vc))
        return state, inter + intra

    s0 = jnp.zeros((B, H, DK, DV), jnp.float32)
    _, o = lax.scan(step, s0, (to_chunks(q), to_chunks(k), to_chunks(v), to_chunks(g)))
    return o.transpose(1, 0, 3, 2, 4).reshape(B, S, H, DV)


def odd_mixer(h, w_in, w_gd, w_gu, b_g, head_g, w_out):
    B, S, _ = h.shape
    z = h @ w_in
    q, k, v, r = jnp.split(z, [GLA_HK, 2 * GLA_HK, 2 * GLA_HK + GLA_HV], axis=-1)
    q = q.reshape(B, S, GLA_HEADS, GLA_DK).astype(jnp.float32) * (GLA_DK ** -0.5)
    k = k.reshape(B, S, GLA_HEADS, GLA_DK).astype(jnp.float32)
    v = v.reshape(B, S, GLA_HEADS, GLA_DV).astype(jnp.float32)

    def gate(d):
        pre = ((h @ w_gd[d]) @ w_gu[d] + b_g[d]).astype(jnp.float32)
        return (jax.nn.log_sigmoid(pre) / GLA_TAU).reshape(B, S, GLA_HEADS, GLA_DK)

    g_f, g_b = gate(0), gate(1)
    o_f = gla_chunked(q, k, v, g_f)
    flip = lambda t: jnp.flip(t, axis=1)
    o_b = flip(gla_chunked(flip(q), flip(k), flip(v), flip(g_b)))
    o = rms_norm(o_f + o_b, head_g).astype(h.dtype)
    o = o * jax.nn.silu(r).reshape(B, S, GLA_HEADS, GLA_DV)
    return o.reshape(B, S, GLA_HV) @ w_out


def cross_attention(h, mem_n, w_q, w_kv, w_o):
    B, S, _ = h.shape
    M = mem_n.shape[1]
    q = (h @ w_q).reshape(B, S, X_HEADS, X_HEAD_DIM)
    kv = (mem_n @ w_kv).reshape(B, M, 2, X_HEADS, X_HEAD_DIM)
    s = jnp.einsum('bqhd,bkhd->bhqk', q, kv[:, :, 0]).astype(jnp.float32) * (X_HEAD_DIM ** -0.5)
    p = jax.nn.softmax(s, axis=-1).astype(h.dtype)
    o = jnp.einsum('bhqk,bkhd->bqhd', p, kv[:, :, 1])
    return o.reshape(B, S, D_MODEL) @ w_o


def conv_ffn(h, w_up, conv_w, conv_b, w_down):
    u = h @ w_up
    up = jnp.pad(u, ((0, 0), (1, 1), (0, 0)))
    u = conv_w[0] * up[:, :-2] + conv_w[1] * up[:, 1:-1] + conv_w[2] * up[:, 2:] + conv_b
    a, g = jnp.split(u, 2, axis=-1)
    return (jax.nn.gelu(g, approximate=True) * a) @ w_down


def trunk(x, mem, norm_g, rel_bias, w_in_even, w_out_even, diff_lambda, diff_norm_g,
          w_in_odd, gla_gate_down, gla_gate_up, gla_gate_bias, gla_norm_g, w_out_odd,
          w_xq, w_xkv, w_xo, w_up, conv_w, conv_b, w_down):
    for l in range(DEPTH):
        ng = norm_g[l]
        i = l // 2
        h = rms_norm(x, ng[0])
        if l % 2 == 0:
            lam_init = 0.8 - 0.6 * math.exp(-0.3 * l)
            mix = even_mixer(h, w_in_even[i], w_out_even[i], diff_lambda[i], diff_norm_g[i],
                             rel_bias, lam_init)
        else:
            mix = odd_mixer(h, w_in_odd[i], gla_gate_down[i], gla_gate_up[i], gla_gate_bias[i],
                            gla_norm_g[i], w_out_odd[i])
        x = x + rms_norm(mix, ng[1])
        mem_n = rms_norm(mem, ng[4])
        x = x + rms_norm(cross_attention(rms_norm(x, ng[2]), mem_n, w_xq[l], w_xkv[l], w_xo[l]), ng[3])
        x = x + rms_norm(conv_ffn(rms_norm(x, ng[5]), w_up[l], conv_w[l], conv_b[l], w_down[l]), ng[6])
    return x


def setup_inputs(seed: int = 0) -> dict:
    key = jax.random.key(seed)
    ks = jax.random.split(key, 24)

    def nrm(k, shape, scale):
        return jax.random.normal(k, shape, jnp.float32) * scale

    return {
        "x_prompt": nrm(ks[0], (BATCH, SEQ, D_MODEL), 1.0),
        "x_sample": nrm(ks[1], (DEC_BATCH, DEC_SEQ, D_MODEL), 1.0),
        "mem_prompt": nrm(ks[2], (BATCH, N_MEM, D_MODEL), 1.0),
        "mem_sample": nrm(ks[3], (DEC_BATCH, N_MEM, D_MODEL), 1.0),
        "norm_g": 1.0 + nrm(ks[4], (DEPTH, 7, D_MODEL), 0.02),
        "rel_bias": nrm(ks[5], (N_BUCKETS, DIFF_HEADS), 0.5),
        "w_in_even": nrm(ks[6], (N_EVEN, D_MODEL, EVEN_IN), D_MODEL ** -0.5),
        "w_out_even": nrm(ks[7], (N_EVEN, EVEN_OUT, D_MODEL), EVEN_OUT ** -0.5),
        "diff_lambda": nrm(ks[8], (N_EVEN, 4, DIFF_HEAD_DIM), 0.1),
        "diff_norm_g": 1.0 + nrm(ks[9], (N_EVEN, DIFF_V_DIM), 0.02),
        "w_in_odd": nrm(ks[10], (N_ODD, D_MODEL, ODD_IN), D_MODEL ** -0.5),
        "gla_gate_down": nrm(ks[11], (N_ODD, 2, D_MODEL, GLA_RANK), D_MODEL ** -0.5),
        "gla_gate_up": nrm(ks[12], (N_ODD, 2, GLA_RANK, GLA_HK), GLA_RANK ** -0.5),
        "gla_gate_bias": nrm(ks[13], (N_ODD, 2, GLA_HK), 0.1),
        "gla_norm_g": 1.0 + nrm(ks[14], (N_ODD, GLA_DV), 0.02),
        "w_out_odd": nrm(ks[15], (N_ODD, GLA_HV, D_MODEL), GLA_HV ** -0.5),
        "w_xq": nrm(ks[16], (DEPTH, D_MODEL, D_MODEL), D_MODEL ** -0.5),
        "w_xkv": nrm(ks[17], (DEPTH, D_MODEL, 2 * D_MODEL), D_MODEL ** -0.5),
        "w_xo": nrm(ks[18], (DEPTH, D_MODEL, D_MODEL), D_MODEL ** -0.5),
        "w_up": nrm(ks[19], (DEPTH, D_MODEL, 2 * D_FF), D_MODEL ** -0.5),
        "conv_w": nrm(ks[20], (DEPTH, 3, 2 * D_FF), 3.0 ** -0.5),
        "conv_b": nrm(ks[21], (DEPTH, 2 * D_FF), 0.02),
        "w_down": nrm(ks[22], (DEPTH, D_FF, D_MODEL), D_FF ** -0.5),
    }


def reference(x_prompt, x_sample, mem_prompt, mem_sample, norm_g, rel_bias, w_in_even, w_out_even,
              diff_lambda, diff_norm_g, w_in_odd, gla_gate_down, gla_gate_up, gla_gate_bias,
              gla_norm_g, w_out_odd, w_xq, w_xkv, w_xo, w_up, conv_w, conv_b, w_down):
    y_prompt = trunk(x_prompt, mem_prompt, norm_g, rel_bias, w_in_even, w_out_even, diff_lambda,
                     diff_norm_g, w_in_odd, gla_gate_down, gla_gate_up, gla_gate_bias, gla_norm_g,
                     w_out_odd, w_xq, w_xkv, w_xo, w_up, conv_w, conv_b, w_down)
    y_sample = trunk(x_sample, mem_sample, norm_g, rel_bias, w_in_even, w_out_even, diff_lambda,
                     diff_norm_g, w_in_odd, gla_gate_down, gla_gate_up, gla_gate_bias, gla_norm_g,
                     w_out_odd, w_xq, w_xkv, w_xo, w_up, conv_w, conv_b, w_down)
    return (y_prompt, y_sample)
```

```python
import functools
import math

import jax
import jax.numpy as jnp
import numpy as np
from jax import lax
from jax.experimental import pallas as pl
from jax.experimental.pallas import tpu as pltpu

F32 = jnp.float32
BF16 = jnp.bfloat16

EPS = 1e-6
DEPTH = 4
FNET_GROUPS = 4
FNET_GROUP_DIM = 256
FNET_WIDTH = FNET_GROUPS * FNET_GROUP_DIM
DIFF_HEADS = 8
DIFF_HEAD_DIM = 64
DIFF_V_DIM = 2 * DIFF_HEAD_DIM
DIFF_QK = DIFF_HEADS * 2 * DIFF_HEAD_DIM
DIFF_WIDTH = DIFF_HEADS * DIFF_V_DIM
N_BUCKETS = 32
MAX_DISTANCE = 128
DIFF_TILE = 256
GLA_HEADS = 4
GLA_DK = 256
GLA_DV = 512
GLA_RANK = 16
GLA_TAU = 16.0
GLA_HK = GLA_HEADS * GLA_DK
GLA_HV = GLA_HEADS * GLA_DV
GLA_CHUNK = 64
GLA_SUB = 16
GLA_ROWS = 256
GATE_PAD = 128
X_HEADS = 4
X_HEAD_DIM = 512
VMEM_LIMIT_BYTES = 52 * 1024 * 1024
BF16_SUBLANES = 16


def _cparams(*sem):
    return pltpu.CompilerParams(dimension_semantics=sem, vmem_limit_bytes=VMEM_LIMIT_BYTES)


def _pick(n, pref, mult=8):
    if n <= pref:
        return n
    for t in range(pref, 0, -1):
        if n % t == 0 and t % mult == 0:
            return t
    return n


def _rms(x, g):
    return x * lax.rsqrt(jnp.mean(x * x, axis=-1, keepdims=True) + EPS) * g


def _prenorm_mm_kernel(x_ref, g_ref, w_ref, o_ref, xn_ref):
    @pl.when(pl.program_id(1) == 0)
    def _():
        xn_ref[...] = _rms(x_ref[...], g_ref[...]).astype(BF16)

    o_ref[...] = jnp.dot(xn_ref[...], w_ref[...], preferred_element_type=F32).astype(o_ref.dtype)


def prenorm_matmul(x, g, w, out_dtype, tm=1024, tn=512):
    M, K = x.shape
    N = w.shape[1]
    tm = _pick(M, tm)
    tn = _pick(N, tn, 128)
    return pl.pallas_call(
        _prenorm_mm_kernel,
        grid=(M // tm, N // tn),
        in_specs=[
            pl.BlockSpec((tm, K), lambda i, j: (i, 0)),
            pl.BlockSpec((1, K), lambda i, j: (0, 0)),
            pl.BlockSpec((K, tn), lambda i, j: (0, j)),
        ],
        out_specs=pl.BlockSpec((tm, tn), lambda i, j: (i, j)),
        out_shape=jax.ShapeDtypeStruct((M, N), out_dtype),
        scratch_shapes=[pltpu.VMEM((tm, K), BF16)],
        compiler_params=_cparams("parallel", "arbitrary"),
    )(x, g.reshape(1, K), w)


def _mm_postnorm_res_kernel(a_ref, w_ref, g_ref, x_ref, o_ref, acc_ref, *, nk):
    k = pl.program_id(1)

    @pl.when(k == 0)
    def _():
        acc_ref[...] = jnp.zeros_like(acc_ref)

    acc_ref[...] += jnp.dot(a_ref[...], w_ref[...], preferred_element_type=F32)

    @pl.when(k == nk - 1)
    def _():
        o_ref[...] = x_ref[...] + _rms(acc_ref[...], g_ref[...])


def matmul_postnorm_residual(a, w, g, x, tm=512, tk=512):
    M, K = a.shape
    D = w.shape[1]
    tm = _pick(M, tm)
    tk = _pick(K, tk, 128)
    nk = K // tk
    return pl.pallas_call(
        functools.partial(_mm_postnorm_res_kernel, nk=nk),
        grid=(M // tm, nk),
        in_specs=[
            pl.BlockSpec((tm, tk), lambda i, k: (i, k)),
            pl.BlockSpec((tk, D), lambda i, k: (k, 0)),
            pl.BlockSpec((1, D), lambda i, k: (0, 0)),
            pl.BlockSpec((tm, D), lambda i, k: (i, 0)),
        ],
        out_specs=pl.BlockSpec((tm, D), lambda i, k: (i, 0)),
        out_shape=jax.ShapeDtypeStruct((M, D), F32),
        scratch_shapes=[pltpu.VMEM((tm, D), F32)],
        compiler_params=_cparams("parallel", "arbitrary"),
    )(a, w, g.reshape(1, D), x)


def _dft_tables(n, scale):
    split = 1
    while split * split < n:
        split *= 2
    hi = n // split
    k = jnp.arange(n, dtype=jnp.int32)[None, :]
    a = jnp.arange(hi, dtype=jnp.int32)[:, None] * split
    b = jnp.arange(split, dtype=jnp.int32)[:, None]
    ang_a = ((a * k) % n).astype(F32) * (2.0 * math.pi / n)
    ang_b = ((b * k) % n).astype(F32) * (2.0 * math.pi / n)
    ca, sa = jnp.cos(ang_a)[:, None, :], jnp.sin(ang_a)[:, None, :]
    cb, sb = jnp.cos(ang_b)[None, :, :], jnp.sin(ang_b)[None, :, :]
    c = (ca * cb - sa * sb) * scale
    s = (sa * cb + ca * sb) * scale
    return c.reshape(n, n), s.reshape(n, n)


def _fnet_channel_kernel(u_ref, w_ref, o_ref):
    y = jnp.dot(u_ref[...], w_ref[...], preferred_element_type=F32)
    o_ref[0] = y[:, :FNET_GROUP_DIM].astype(BF16)
    o_ref[1] = y[:, FNET_GROUP_DIM:].astype(BF16)


def fnet_channel_dft(z, w_ch, tm=1024):
    M = z.shape[0]
    tm = _pick(M, tm)
    gd = FNET_GROUP_DIM
    return pl.pallas_call(
        _fnet_channel_kernel,
        grid=(M // tm, FNET_GROUPS),
        in_specs=[
            pl.BlockSpec((tm, gd), lambda i, g: (i, g)),
            pl.BlockSpec((gd, 2 * gd), lambda i, g: (0, 0)),
        ],
        out_specs=pl.BlockSpec((2, tm, gd), lambda i, g: (0, i, g)),
        out_shape=jax.ShapeDtypeStruct((2, M, FNET_WIDTH), BF16),
        compiler_params=_cparams("parallel", "parallel"),
    )(z, w_ch)


def _fnet_seq_kernel(f_ref, y_ref, o_ref, acc_ref, *, nk):
    k = pl.program_id(2)

    @pl.when(k == 0)
    def _():
        acc_ref[...] = jnp.zeros_like(acc_ref)

    acc_ref[...] += jnp.dot(f_ref[...], y_ref[...], preferred_element_type=F32)

    @pl.when(k == nk - 1)
    def _():
        o_ref[...] = acc_ref[...].astype(o_ref.dtype)


def fnet_seq_dft(fcat, y, out, row0, nb, S, tm=1024, tk=512):
    tm = _pick(S, tm)
    tk = _pick(S, tk, 128)
    nks = S // tk
    nk = 2 * nks
    C = y.shape[2]
    rb_k = row0 // tk
    rb_m = row0 // tm

    def kernel(f_ref, y_ref, prev_ref, o_ref, acc_ref):
        del prev_ref
        _fnet_seq_kernel(f_ref, y_ref, o_ref, acc_ref, nk=nk)

    return pl.pallas_call(
        kernel,
        grid=(nb, S // tm, nk),
        in_specs=[
            pl.BlockSpec((tm, tk), lambda b, i, k: (i, k)),
            pl.BlockSpec((None, tk, C), lambda b, i, k: (k // nks, rb_k + b * nks + k % nks, 0)),
            pl.BlockSpec(memory_space=pl.ANY),
        ],
        out_specs=pl.BlockSpec((tm, C), lambda b, i, k: (rb_m + b * (S // tm) + i, 0)),
        out_shape=jax.ShapeDtypeStruct(out.shape, out.dtype),
        scratch_shapes=[pltpu.VMEM((tm, C), F32)],
        input_output_aliases={2: 0},
        compiler_params=_cparams("parallel", "parallel", "arbitrary"),
    )(fcat, y, out)


def _t5_bucket(rel):
    n = N_BUCKETS // 2
    max_exact = n // 2
    base = jnp.where(rel > 0, n, 0)
    a = jnp.abs(rel)
    af = jnp.maximum(a, 1).astype(F32)
    large = max_exact + (jnp.log(af / max_exact) / math.log(MAX_DISTANCE / max_exact)
                         * (n - max_exact)).astype(jnp.int32)
    large = jnp.minimum(large, n - 1)
    return base + jnp.where(a < max_exact, a, large)


def _bias_tiles(rel_bias):
    t = DIFF_TILE
    d = jnp.arange(-2, 3, dtype=jnp.int32)[:, None, None] * t
    qq = jnp.arange(t, dtype=jnp.int32)[None, :, None]
    kk = jnp.arange(t, dtype=jnp.int32)[None, None, :]
    tiles = rel_bias[_t5_bucket(d + kk - qq)]
    return tiles.transpose(0, 3, 1, 2).astype(F32)


def _diff_attn_kernel(q_ref, k_ref, v_ref, bias_ref, lam_ref, hg_ref, o_ref, m_ref, l_ref, acc_ref,
                      *, nk, lam_init):
    j = pl.program_id(3)

    @pl.when(j == 0)
    def _():
        m_ref[...] = jnp.full_like(m_ref, -jnp.inf)
        l_ref[...] = jnp.zeros_like(l_ref)
        acc_ref[...] = jnp.zeros_like(acc_ref)

    q = q_ref[...]
    k = k_ref[...]
    v = v_ref[...]
    bias = bias_ref[...]
    lane = lax.broadcasted_iota(jnp.int32, q.shape, 1)
    scale = jnp.asarray(DIFF_HEAD_DIM ** -0.5, BF16)
    zero = jnp.zeros_like(q)
    halves = (jnp.where(lane < DIFF_HEAD_DIM, q, zero) * scale,
              jnp.where(lane >= DIFF_HEAD_DIM, q, zero) * scale)
    for c, qc in enumerate(halves):
        s = lax.dot_general(qc, k, (((1,), (1,)), ((), ())), preferred_element_type=F32) + bias
        m_prev = m_ref[c]
        m_new = jnp.maximum(m_prev, jnp.max(s, axis=-1, keepdims=True))
        alpha = jnp.exp(m_prev - m_new)
        p = jnp.exp(s - m_new)
        l_ref[c] = alpha * l_ref[c] + jnp.sum(p, axis=-1, keepdims=True)
        acc_ref[c] = alpha * acc_ref[c] + jnp.dot(p.astype(BF16), v, preferred_element_type=F32)
        m_ref[c] = m_new

    @pl.when(j == nk - 1)
    def _():
        lp = lam_ref[...]
        lam = (jnp.exp(jnp.sum(lp[0:1] * lp[1:2], axis=-1, keepdims=True))
               - jnp.exp(jnp.sum(lp[2:3] * lp[3:4], axis=-1, keepdims=True)) + lam_init)
        o = acc_ref[0] / l_ref[0] - lam * (acc_ref[1] / l_ref[1])
        o_ref[...] = (_rms(o, hg_ref[...]) * (1.0 - lam_init)).astype(o_ref.dtype)


def diff_attention(z, bias5, lam_p, head_g, out, row0, nb, S, lam_init):
    t = DIFF_TILE if S % DIFF_TILE == 0 else S
    assert t >= MAX_DISTANCE or t == S
    nq = S // t
    rb = row0 // t
    vd = DIFF_V_DIM
    q0 = FNET_WIDTH // vd
    k0 = (FNET_WIDTH + DIFF_QK) // vd
    v0 = (FNET_WIDTH + 2 * DIFF_QK) // vd
    o0 = FNET_WIDTH // vd

    def kernel(q_ref, k_ref, v_ref, bias_ref, lam_ref, hg_ref, prev_ref, o_ref, m_ref, l_ref, acc_ref):
        del prev_ref
        _diff_attn_kernel(q_ref, k_ref, v_ref, bias_ref, lam_ref, hg_ref, o_ref, m_ref, l_ref, acc_ref,
                          nk=nq, lam_init=lam_init)

    return pl.pallas_call(
        kernel,
        grid=(nb, DIFF_HEADS, nq, nq),
        in_specs=[
            pl.BlockSpec((t, vd), lambda b, h, i, j: (rb + b * nq + i, q0 + h)),
            pl.BlockSpec((t, vd), lambda b, h, i, j: (rb + b * nq + j, k0 + h)),
            pl.BlockSpec((t, vd), lambda b, h, i, j: (rb + b * nq + j, v0 + h)),
            pl.BlockSpec((None, None, t, t), lambda b, h, i, j: (jnp.clip(j - i, -2, 2) + 2, h, 0, 0)),
            pl.BlockSpec((4, DIFF_HEAD_DIM), lambda b, h, i, j: (0, 0)),
            pl.BlockSpec((1, vd), lambda b, h, i, j: (0, 0)),
            pl.BlockSpec(memory_space=pl.ANY),
        ],
        out_specs=pl.BlockSpec((t, vd), lambda b, h, i, j: (rb + b * nq + i, o0 + h)),
        out_shape=jax.ShapeDtypeStruct(out.shape, out.dtype),
        scratch_shapes=[
            pltpu.VMEM((2, t, 1), F32),
            pltpu.VMEM((2, t, 1), F32),
            pltpu.VMEM((2, t, vd), F32),
        ],
        input_output_aliases={6: 0},
        compiler_params=_cparams("parallel", "parallel", "parallel", "arbitrary"),
    )(z, z, z, bias5, lam_p.astype(F32), head_g.reshape(1, vd).astype(F32), out)


def _log_sigmoid(x):
    return jnp.minimum(x, 0.0) - jnp.log(1.0 + jnp.exp(-jnp.abs(x)))


def _gla_chunk(q, k, v, pre, st):
    C = GLA_CHUNK
    nt = (((1,), (1,)), ((), ()))
    g = _log_sigmoid(pre) * (1.0 / GLA_TAU)
    row = lax.broadcasted_iota(jnp.int32, (C, C), 0)
    col = lax.broadcasted_iota(jnp.int32, (C, C), 1)
    tri = jnp.where(row >= col, 1.0, 0.0).astype(BF16)
    g_hi = g.astype(BF16)
    g_lo = (g - g_hi.astype(F32)).astype(BF16)
    b = (jnp.dot(tri, g_hi, preferred_element_type=F32)
         + jnp.dot(tri, g_lo, preferred_element_type=F32))
    b_last = b[C - 1:C, :]

    qe = (q * jnp.exp(b)).astype(BF16)
    inter = lax.dot_general(qe, st.astype(BF16), nt, preferred_element_type=F32)

    sub = GLA_SUB
    lane = lax.broadcasted_iota(jnp.int32, (sub, C), 1)
    trow = lax.broadcasted_iota(jnp.int32, (sub, 1), 0)
    pieces = []
    for j in range(C // sub):
        lo = j * sub
        qj = q[lo:lo + sub]
        bj = b[lo:lo + sub]
        aj = jnp.zeros((sub, C), F32)
        for s in range(sub):
            ks = k[lo + s:lo + s + 1]
            bs = b[lo + s:lo + s + 1]
            e = jnp.exp(jnp.where(trow >= s, bj - bs, -jnp.inf))
            colv = jnp.sum(qj * ks * e, axis=-1, keepdims=True)
            aj = aj + jnp.where(lane == lo + s, colv, 0.0)
        if j > 0:
            mj = b[lo - 1:lo]
            qt = (qj * jnp.exp(bj - mj)).astype(BF16)
            kt = (k * jnp.exp(jnp.minimum(mj - b, 0.0))).astype(BF16)
            off = lax.dot_general(qt, kt, nt, preferred_element_type=F32)
            aj = aj + jnp.where(lane < lo, off, 0.0)
        pieces.append(aj)
    attn = jnp.concatenate(pieces, axis=0)
    intra = jnp.dot(attn.astype(BF16), v, preferred_element_type=F32)

    kd = (k * jnp.exp(b_last - b)).astype(BF16)
    vt = v.astype(F32).T.astype(BF16)
    st_new = st * jnp.exp(b_last) + jnp.dot(vt, kd, preferred_element_type=F32)
    return inter + intra, st_new


def _gla_kernel(q_ref, k_ref, v_ref, lr_ref, wgu_ref, bg_ref, o_ref, st_ref, *, rows):
    @pl.when(pl.program_id(2) == 0)
    def _():
        st_ref[...] = jnp.zeros_like(st_ref)

    wgu = wgu_ref[...]
    bg = bg_ref[...]

    def step(c, carry):
        r0 = pl.multiple_of(c * GLA_CHUNK, GLA_CHUNK)
        sl = pl.ds(r0, GLA_CHUNK)
        q = q_ref[sl, :].astype(F32) * (GLA_DK ** -0.5)
        k = k_ref[sl, :].astype(F32)
        v = v_ref[sl, :]
        pre = jnp.dot(lr_ref[sl, :].astype(BF16), wgu, preferred_element_type=F32) + bg
        out, st_new = _gla_chunk(q, k, v, pre, st_ref[...])
        o_ref[sl, :] = out
        st_ref[...] = st_new
        return carry

    lax.fori_loop(0, rows // GLA_CHUNK, step, 0)


def gla_direction(z, lr, wgu, bg, row0, nb, S, M):
    rows = _pick(S, GLA_ROWS, GLA_CHUNK)
    nr = S // rows
    rb = row0 // rows
    kb = GLA_HK // GLA_DK
    vb = 2 * GLA_HK // GLA_DV
    return pl.pallas_call(
        functools.partial(_gla_kernel, rows=rows),
        grid=(nb, GLA_HEADS, nr),
        in_specs=[
            pl.BlockSpec((rows, GLA_DK), lambda b, h, c: (rb + b * nr + c, h)),
            pl.BlockSpec((rows, GLA_DK), lambda b, h, c: (rb + b * nr + c, kb + h)),
            pl.BlockSpec((rows, GLA_DV), lambda b, h, c: (rb + b * nr + c, vb + h)),
            pl.BlockSpec((rows, GATE_PAD), lambda b, h, c: (rb + b * nr + c, 0)),
            pl.BlockSpec((GATE_PAD, GLA_DK), lambda b, h, c: (0, h)),
            pl.BlockSpec((1, GLA_DK), lambda b, h, c: (0, h)),
        ],
        out_specs=pl.BlockSpec((rows, GLA_DV), lambda b, h, c: (b * nr + c, h)),
        out_shape=jax.ShapeDtypeStruct((nb * S, GLA_HV), F32),
        scratch_shapes=[pltpu.VMEM((GLA_DV, GLA_DK), F32)],
        compiler_params=_cparams("parallel", "parallel", "arbitrary"),
    )(z, z, z, lr, wgu, bg)


def _gla_post_kernel(of_ref, ob_ref, r_ref, g_ref, o_ref):
    o = _rms(of_ref[...] + ob_ref[...], g_ref[...])
    r = r_ref[...].astype(F32)
    o_ref[...] = (o * (r / (1.0 + jnp.exp(-r)))).astype(o_ref.dtype)


def gla_post(o_f, o_b, z, head_g, tm=512):
    M = o_f.shape[0]
    tm = _pick(M, tm)
    r0 = (2 * GLA_HK + GLA_HV) // GLA_DV
    return pl.pallas_call(
        _gla_post_kernel,
        grid=(M // tm, GLA_HEADS),
        in_specs=[
            pl.BlockSpec((tm, GLA_DV), lambda i, h: (i, h)),
            pl.BlockSpec((tm, GLA_DV), lambda i, h: (i, h)),
            pl.BlockSpec((tm, GLA_DV), lambda i, h: (i, r0 + h)),
            pl.BlockSpec((1, GLA_DV), lambda i, h: (0, 0)),
        ],
        out_specs=pl.BlockSpec((tm, GLA_DV), lambda i, h: (i, h)),
        out_shape=jax.ShapeDtypeStruct((M, GLA_HV), BF16),
        compiler_params=_cparams("parallel", "parallel"),
    )(o_f, o_b, z, head_g.reshape(1, GLA_DV).astype(F32))


def _cross_attn_kernel(q_ref, k_ref, v_ref, o_ref):
    s = lax.dot_general(q_ref[...], k_ref[...], (((1,), (1,)), ((), ())),
                        preferred_element_type=F32) * (X_HEAD_DIM ** -0.5)
    m = jnp.max(s, axis=-1, keepdims=True)
    p = jnp.exp(s - m)
    p = p / jnp.sum(p, axis=-1, keepdims=True)
    o_ref[...] = jnp.dot(p.astype(BF16), v_ref[...], preferred_element_type=F32).astype(o_ref.dtype)


def cross_attention(q, kv, out, row0, nb, S, mem_b0, n_mem, tq=512):
    tq = _pick(S, tq)
    nq = S // tq
    rb = row0 // tq
    hd = X_HEAD_DIM

    def kernel(q_ref, k_ref, v_ref, prev_ref, o_ref):
        del prev_ref
        _cross_attn_kernel(q_ref, k_ref, v_ref, o_ref)

    return pl.pallas_call(
        kernel,
        grid=(nb, nq, X_HEADS),
        in_specs=[
            pl.BlockSpec((tq, hd), lambda b, i, h: (rb + b * nq + i, h)),
            pl.BlockSpec((n_mem, hd), lambda b, i, h: (mem_b0 + b, h)),
            pl.BlockSpec((n_mem, hd), lambda b, i, h: (mem_b0 + b, X_HEADS + h)),
            pl.BlockSpec(memory_space=pl.ANY),
        ],
        out_specs=pl.BlockSpec((tq, hd), lambda b, i, h: (rb + b * nq + i, h)),
        out_shape=jax.ShapeDtypeStruct(out.shape, out.dtype),
        input_output_aliases={3: 0},
        compiler_params=_cparams("parallel", "parallel", "parallel"),
    )(q, kv, kv, out)


def _gelu_tanh(x):
    return 0.5 * x * (1.0 + jnp.tanh(math.sqrt(2.0 / math.pi) * (x + 0.044715 * x * x * x)))


def _conv_gate_kernel(ua_ref, uap_ref, uan_ref, ug_ref, ugp_ref, ugn_ref, wa_ref, wg_ref, ba_ref, bg_ref,
                      o_ref, *, nt):
    i = pl.program_id(1)
    first = i == 0
    last = i == nt - 1

    def conv(u_ref, p_ref, n_ref, w_ref, b_ref):
        u = u_ref[...].astype(F32)
        tm = u.shape[0]
        prev = jnp.where(first, 0.0, p_ref[BF16_SUBLANES - 1:BF16_SUBLANES, :].astype(F32))
        nxt = jnp.where(last, 0.0, n_ref[0:1, :].astype(F32))
        rows = lax.broadcasted_iota(jnp.int32, (tm, 1), 0)
        up = jnp.where(rows == 0, prev, pltpu.roll(u, 1, 0))
        un = jnp.where(rows == tm - 1, nxt, pltpu.roll(u, tm - 1, 0))
        w = w_ref[...]
        return w[0:1] * up + w[1:2] * u + w[2:3] * un + b_ref[...]

    a = conv(ua_ref, uap_ref, uan_ref, wa_ref, ba_ref)
    g = conv(ug_ref, ugp_ref, ugn_ref, wg_ref, bg_ref)
    o_ref[...] = (_gelu_tanh(g) * a).astype(o_ref.dtype)


def conv_gate(u, conv_w, conv_b, out, row0, nb, S, tm=512, tn=512):
    M, F2 = u.shape
    F = F2 // 2
    tm = _pick(S, tm, BF16_SUBLANES)
    tn = _pick(F, tn, 128)
    nt = S // tm
    nj = F // tn
    rb = row0 // tm
    hb = tm // BF16_SUBLANES
    h_last = M // BF16_SUBLANES - 1

    def main(off):
        return pl.BlockSpec((tm, tn), lambda b, i, j: (rb + b * nt + i, off + j))

    def prev(off):
        return pl.BlockSpec((BF16_SUBLANES, tn),
                            lambda b, i, j: (jnp.maximum((rb + b * nt + i) * hb - 1, 0), off + j))

    def nxt(off):
        return pl.BlockSpec((BF16_SUBLANES, tn),
                            lambda b, i, j: (jnp.minimum((rb + b * nt + i + 1) * hb, h_last), off + j))

    def cols(rows, off):
        return pl.BlockSpec((rows, tn), lambda b, i, j: (0, off + j))

    def kernel(ua, uap, uan, ug, ugp, ugn, wa, wg, ba, bg, prev_ref, o_ref):
        del prev_ref
        _conv_gate_kernel(ua, uap, uan, ug, ugp, ugn, wa, wg, ba, bg, o_ref, nt=nt)

    return pl.pallas_call(
        kernel,
        grid=(nb, nt, nj),
        in_specs=[main(0), prev(0), nxt(0), main(nj), prev(nj), nxt(nj),
                  cols(3, 0), cols(3, nj), cols(1, 0), cols(1, nj),
                  pl.BlockSpec(memory_space=pl.ANY)],
        out_specs=pl.BlockSpec((tm, tn), lambda b, i, j: (rb + b * nt + i, j)),
        out_shape=jax.ShapeDtypeStruct(out.shape, out.dtype),
        input_output_aliases={10: 0},
        compiler_params=_cparams("parallel", "parallel", "parallel"),
    )(u, u, u, u, u, u, conv_w, conv_w, conv_b.reshape(1, F2), conv_b.reshape(1, F2), out)


def _flip_seqs(a, segs):
    parts = []
    for row0, nb, S in segs:
        blk = a[row0:row0 + nb * S].reshape(nb, S, a.shape[-1])
        parts.append(jnp.flip(blk, axis=1).reshape(nb * S, a.shape[-1]))
    return jnp.concatenate(parts, axis=0)


def kernel(x_prompt, x_sample, mem_prompt, mem_sample, norm_g, rel_bias, w_in_even, w_out_even,
           diff_lambda, diff_norm_g, w_in_odd, gla_gate_down, gla_gate_up, gla_gate_bias, gla_norm_g,
           w_out_odd, w_xq, w_xkv, w_xo, w_up, conv_w, conv_b, w_down):
    bp, sp, D = x_prompt.shape
    bs, ss, _ = x_sample.shape
    n_mem = mem_prompt.shape[1]
    segs = ((0, bp, sp), (bp * sp, bs, ss))
    M = bp * sp + bs * ss
    x = jnp.concatenate([x_prompt.reshape(bp * sp, D), x_sample.reshape(bs * ss, D)], axis=0)
    mem = jnp.concatenate([mem_prompt.reshape(bp * n_mem, D), mem_sample.reshape(bs * n_mem, D)], axis=0)
    mem_b0 = (0, bp)
    norm_g = norm_g.astype(F32)

    bias5 = _bias_tiles(rel_bias)
    gd = FNET_GROUP_DIM
    c_ch, s_ch = _dft_tables(gd, gd ** -0.5)
    w_ch = jnp.concatenate([c_ch, -s_ch], axis=1).astype(BF16)
    fcat = {}
    for _, _, S in segs:
        if S not in fcat:
            c_s, s_s = _dft_tables(S, S ** -0.5)
            fcat[S] = jnp.concatenate([c_s, s_s], axis=1).astype(BF16)

    for l in range(DEPTH):
        ng = norm_g[l]
        i = l // 2
        if l % 2 == 0:
            lam_init = 0.8 - 0.6 * math.exp(-0.3 * l)
            z = prenorm_matmul(x, ng[0], w_in_even[i].astype(BF16), BF16)
            y = fnet_channel_dft(z, w_ch)
            mixed = jnp.zeros((M, FNET_WIDTH + DIFF_WIDTH), BF16)
            for row0, nb, S in segs:
                mixed = fnet_seq_dft(fcat[S], y, mixed, row0, nb, S)
                mixed = diff_attention(z, bias5, diff_lambda[i], diff_norm_g[i], mixed, row0, nb, S, lam_init)
            x = matmul_postnorm_residual(mixed, w_out_even[i].astype(BF16), ng[1], x)
        else:
            z = prenorm_matmul(x, ng[0], w_in_odd[i].astype(BF16), BF16)
            w_gd = jnp.concatenate([gla_gate_down[i, 0], gla_gate_down[i, 1]], axis=1)
            w_gd = jnp.pad(w_gd, ((0, 0), (0, GATE_PAD - 2 * GLA_RANK))).astype(BF16)
            lr = prenorm_matmul(x, ng[0], w_gd, F32)
            outs = []
            for d in range(2):
                wgu = jnp.zeros((GATE_PAD, GLA_HK), F32)
                wgu = wgu.at[d * GLA_RANK:(d + 1) * GLA_RANK].set(gla_gate_up[i, d]).astype(BF16)
                bg = gla_gate_bias[i, d].reshape(1, GLA_HK).astype(F32)
                zd, lrd = (z, lr) if d == 0 else (_flip_seqs(z, segs), _flip_seqs(lr, segs))
                o = jnp.concatenate([gla_direction(zd, lrd, wgu, bg, row0, nb, S, M)
                                     for row0, nb, S in segs], axis=0)
                outs.append(o if d == 0 else _flip_seqs(o, segs))
            mixed = gla_post(outs[0], outs[1], z, gla_norm_g[i])
            x = matmul_postnorm_residual(mixed, w_out_odd[i].astype(BF16), ng[1], x)

        q = prenorm_matmul(x, ng[2], w_xq[l].astype(BF16), BF16)
        kv = prenorm_matmul(mem, ng[4], w_xkv[l].astype(BF16), BF16)
        att = jnp.zeros((M, D), BF16)
        for (row0, nb, S), b0 in zip(segs, mem_b0):
            att = cross_attention(q, kv, att, row0, nb, S, b0, n_mem)
        x = matmul_postnorm_residual(att, w_xo[l].astype(BF16), ng[3], x)

        u = prenorm_matmul(x, ng[5], w_up[l].astype(BF16), BF16)
        act = jnp.zeros((M, w_down.shape[1]), BF16)
        for row0, nb, S in segs:
            act = conv_gate(u, conv_w[l].astype(F32), conv_b[l].astype(F32), act, row0, nb, S)
        x = matmul_postnorm_residual(act, w_down[l].astype(BF16), ng[6], x)

    y_prompt = x[:bp * sp].reshape(bp, sp, D)
    y_sample = x[bp * sp:].reshape(bs, ss, D)
    return (y_prompt, y_sample)
```

```python
import functools
import math

import jax
import jax.numpy as jnp
from jax import lax
from jax.experimental import pallas as pl
from jax.experimental.pallas import tpu as pltpu

F32 = jnp.float32
BF16 = jnp.bfloat16

EPS = 1e-6
DEPTH = 4
LOG2E = math.log2(math.e)
FNET_GROUPS = 4
FNET_GROUP_DIM = 256
FNET_WIDTH = FNET_GROUPS * FNET_GROUP_DIM
DIFF_HEADS = 8
DIFF_HEAD_DIM = 64
DIFF_V_DIM = 2 * DIFF_HEAD_DIM
DIFF_QK = DIFF_HEADS * 2 * DIFF_HEAD_DIM
DIFF_WIDTH = DIFF_HEADS * DIFF_V_DIM
N_BUCKETS = 32
MAX_DISTANCE = 128
DIFF_TQ = 256
DIFF_TK = 512
GLA_HEADS = 4
GLA_DK = 256
GLA_DV = 512
GLA_RANK = 16
GLA_TAU = 16.0
GLA_HK = GLA_HEADS * GLA_DK
GLA_HV = GLA_HEADS * GLA_DV
GLA_CHUNK = 64
GLA_SUB = 16
GLA_ROWS = 256
GATE_PAD = 128
X_HEADS = 4
X_HEAD_DIM = 512
VMEM_LIMIT_BYTES = 52 * 1024 * 1024
BF16_SUBLANES = 16


def _cparams(*sem):
    return pltpu.CompilerParams(dimension_semantics=sem, vmem_limit_bytes=VMEM_LIMIT_BYTES)


def _pick(n, pref, mult=8):
    if n <= pref:
        return n
    for t in range(pref, 0, -1):
        if n % t == 0 and t % mult == 0:
            return t
    return n


def _rms(x, g):
    return x * lax.rsqrt(jnp.mean(x * x, axis=-1, keepdims=True) + EPS) * g


def _call_into(body, out, args, in_specs, **kw):
    if out is None:
        return pl.pallas_call(body, in_specs=in_specs, **kw)(*args)
    n = len(args)

    def aliased(*refs):
        body(*refs[:n], *refs[n + 1:])

    return pl.pallas_call(aliased, in_specs=list(in_specs) + [pl.BlockSpec(memory_space=pl.ANY)],
                          input_output_aliases={n: 0}, **kw)(*args, out)


def _prenorm_mm_kernel(x_ref, g_ref, w_ref, o_ref, xn_ref):
    @pl.when(pl.program_id(1) == 0)
    def _():
        xn_ref[...] = _rms(x_ref[...], g_ref[...]).astype(BF16)

    o_ref[...] = jnp.dot(xn_ref[...], w_ref[...], preferred_element_type=F32).astype(o_ref.dtype)


def prenorm_matmul(x, g, w, out_dtype, tm=1024, tn=512):
    M, K = x.shape
    N = w.shape[1]
    tm = _pick(M, tm)
    tn = _pick(N, tn, 128)
    return pl.pallas_call(
        _prenorm_mm_kernel,
        grid=(M // tm, N // tn),
        in_specs=[
            pl.BlockSpec((tm, K), lambda i, j: (i, 0)),
            pl.BlockSpec((1, K), lambda i, j: (0, 0)),
            pl.BlockSpec((K, tn), lambda i, j: (0, j)),
        ],
        out_specs=pl.BlockSpec((tm, tn), lambda i, j: (i, j)),
        out_shape=jax.ShapeDtypeStruct((M, N), out_dtype),
        scratch_shapes=[pltpu.VMEM((tm, K), BF16)],
        compiler_params=_cparams("parallel", "arbitrary"),
        name="prenorm_mm",
    )(x, g.reshape(1, K), w)


def _mm_postnorm_res_kernel(a_ref, w_ref, g_ref, x_ref, o_ref, acc_ref, *, nk):
    k = pl.program_id(1)

    @pl.when(k == 0)
    def _():
        acc_ref[...] = jnp.zeros_like(acc_ref)

    acc_ref[...] += jnp.dot(a_ref[...], w_ref[...], preferred_element_type=F32)

    @pl.when(k == nk - 1)
    def _():
        o_ref[...] = x_ref[...] + _rms(acc_ref[...], g_ref[...])


def matmul_postnorm_residual(a, w, g, x, tm=512, tk=512):
    M, K = a.shape
    D = w.shape[1]
    tm = _pick(M, tm)
    tk = _pick(K, tk, 128)
    nk = K // tk
    return pl.pallas_call(
        functools.partial(_mm_postnorm_res_kernel, nk=nk),
        grid=(M // tm, nk),
        in_specs=[
            pl.BlockSpec((tm, tk), lambda i, k: (i, k)),
            pl.BlockSpec((tk, D), lambda i, k: (k, 0)),
            pl.BlockSpec((1, D), lambda i, k: (0, 0)),
            pl.BlockSpec((tm, D), lambda i, k: (i, 0)),
        ],
        out_specs=pl.BlockSpec((tm, D), lambda i, k: (i, 0)),
        out_shape=jax.ShapeDtypeStruct((M, D), F32),
        scratch_shapes=[pltpu.VMEM((tm, D), F32)],
        compiler_params=_cparams("parallel", "arbitrary"),
        name="mm_postnorm_res",
    )(a, w, g.reshape(1, D), x)


def _dft_tables(n, scale):
    split = 1
    while split * split < n:
        split *= 2
    hi = n // split
    k = jnp.arange(n, dtype=jnp.int32)[None, :]
    a = jnp.arange(hi, dtype=jnp.int32)[:, None] * split
    b = jnp.arange(split, dtype=jnp.int32)[:, None]
    ang_a = ((a * k) % n).astype(F32) * (2.0 * math.pi / n)
    ang_b = ((b * k) % n).astype(F32) * (2.0 * math.pi / n)
    ca, sa = jnp.cos(ang_a)[:, None, :], jnp.sin(ang_a)[:, None, :]
    cb, sb = jnp.cos(ang_b)[None, :, :], jnp.sin(ang_b)[None, :, :]
    c = (ca * cb - sa * sb) * scale
    s = (sa * cb + ca * sb) * scale
    return c.reshape(n, n), s.reshape(n, n)


def _fnet_channel_kernel(u_ref, w_ref, o_ref):
    y = jnp.dot(u_ref[...], w_ref[...], preferred_element_type=F32)
    o_ref[0] = y[:, :FNET_GROUP_DIM].astype(BF16)
    o_ref[1] = y[:, FNET_GROUP_DIM:].astype(BF16)


def fnet_channel_dft(z, w_ch, tm=1024):
    M = z.shape[0]
    tm = _pick(M, tm)
    gd = FNET_GROUP_DIM
    return pl.pallas_call(
        _fnet_channel_kernel,
        grid=(M // tm, FNET_GROUPS),
        in_specs=[
            pl.BlockSpec((tm, gd), lambda i, g: (i, g)),
            pl.BlockSpec((gd, 2 * gd), lambda i, g: (0, 0)),
        ],
        out_specs=pl.BlockSpec((2, tm, gd), lambda i, g: (0, i, g)),
        out_shape=jax.ShapeDtypeStruct((2, M, FNET_WIDTH), BF16),
        compiler_params=_cparams("parallel", "parallel"),
        name="fnet_channel",
    )(z, w_ch)


def _fnet_seq_kernel(f_ref, y_ref, o_ref, acc_ref, *, nk):
    k = pl.program_id(2)

    @pl.when(k == 0)
    def _():
        acc_ref[...] = jnp.zeros_like(acc_ref)

    acc_ref[...] += jnp.dot(f_ref[...], y_ref[...], preferred_element_type=F32)

    @pl.when(k == nk - 1)
    def _():
        o_ref[...] = acc_ref[...].astype(o_ref.dtype)


def fnet_seq_dft(fcat, y, out, out_shape, row0, nb, S, tm=1024, tk=512):
    tm = _pick(S, tm)
    tk = _pick(S, tk, 128)
    nks = S // tk
    nk = 2 * nks
    C = y.shape[2]
    rb_k = row0 // tk
    rb_m = row0 // tm
    return _call_into(
        functools.partial(_fnet_seq_kernel, nk=nk), out, (fcat, y),
        in_specs=[
            pl.BlockSpec((tm, tk), lambda b, i, k: (i, k)),
            pl.BlockSpec((None, tk, C), lambda b, i, k: (k // nks, rb_k + b * nks + k % nks, 0)),
        ],
        grid=(nb, S // tm, nk),
        out_specs=pl.BlockSpec((tm, C), lambda b, i, k: (rb_m + b * (S // tm) + i, 0)),
        out_shape=out_shape,
        scratch_shapes=[pltpu.VMEM((tm, C), F32)],
        compiler_params=_cparams("parallel", "parallel", "arbitrary"),
        name="fnet_seq",
    )


def _t5_bucket(rel):
    n = N_BUCKETS // 2
    max_exact = n // 2
    base = jnp.where(rel > 0, n, 0)
    a = jnp.abs(rel)
    af = jnp.maximum(a, 1).astype(F32)
    large = max_exact + (jnp.log(af / max_exact) / math.log(MAX_DISTANCE / max_exact)
                         * (n - max_exact)).astype(jnp.int32)
    large = jnp.minimum(large, n - 1)
    return base + jnp.where(a < max_exact, a, large)


def _bias_tile_kernel(rb_ref, bucket_ref, o_ref):
    h = pl.program_id(1)
    bucket = bucket_ref[...]
    acc = jnp.zeros(bucket.shape, F32)
    for n in range(N_BUCKETS):
        acc = jnp.where(bucket == n, rb_ref[n, h], acc)
    o_ref[...] = acc * LOG2E


def bias_tiles(rel_bias, tq, tk):
    r = tk // tq
    assert tq >= MAX_DISTANCE and tk % tq == 0
    n = 2 * r + 2
    d = jnp.arange(-r - 1, r + 1, dtype=jnp.int32)[:, None, None] * tq
    qq = jnp.arange(tq, dtype=jnp.int32)[None, :, None]
    kk = jnp.arange(tk, dtype=jnp.int32)[None, None, :]
    bucket = _t5_bucket(d + kk - qq)
    return pl.pallas_call(
        _bias_tile_kernel,
        grid=(n, DIFF_HEADS),
        in_specs=[
            pl.BlockSpec(memory_space=pltpu.SMEM),
            pl.BlockSpec((None, tq, tk), lambda t, h: (t, 0, 0)),
        ],
        out_specs=pl.BlockSpec((None, None, tq, tk), lambda t, h: (t, h, 0, 0)),
        out_shape=jax.ShapeDtypeStruct((n, DIFF_HEADS, tq, tk), F32),
        compiler_params=_cparams("parallel", "parallel"),
        name="bias_tiles",
    )(rel_bias.astype(F32), bucket)


def _diff_attn_kernel(q_ref, k_ref, v_ref, bias_ref, lam_ref, hg_ref, o_ref, m_ref, l_ref, acc_ref,
                      *, nkc, tq, tk, lam_init):
    i = pl.program_id(2)
    r = tk // tq
    m_ref[...] = jnp.full_like(m_ref, -jnp.inf)
    l_ref[...] = jnp.zeros_like(l_ref)
    acc_ref[...] = jnp.zeros_like(acc_ref)

    q = q_ref[...]
    lane = lax.broadcasted_iota(jnp.int32, q.shape, 1)
    zero = jnp.zeros_like(q)
    halves = (jnp.where(lane < DIFF_HEAD_DIM, q, zero), jnp.where(lane >= DIFF_HEAD_DIM, q, zero))

    def chunk(j, carry):
        rows = pl.ds(pl.multiple_of(j * tk, tk), tk)
        k = k_ref[rows, :]
        v = v_ref[rows, :]
        bias = bias_ref[jnp.clip(j * r - i, -r - 1, r) + r + 1]
        for c, qc in enumerate(halves):
            s = lax.dot_general(qc, k, (((1,), (1,)), ((), ())), preferred_element_type=F32) + bias
            m_prev = m_ref[c]
            m_new = jnp.maximum(m_prev, jnp.max(s, axis=-1, keepdims=True))
            alpha = jnp.exp2(m_prev - m_new)
            p = jnp.exp2(s - m_new)
            l_ref[c] = alpha * l_ref[c] + jnp.sum(p, axis=-1, keepdims=True)
            acc_ref[c] = alpha * acc_ref[c] + jnp.dot(p.astype(BF16), v, preferred_element_type=F32)
            m_ref[c] = m_new
        return carry

    lax.fori_loop(0, nkc, chunk, 0)

    lp = lam_ref[...]
    lam = (jnp.exp(jnp.sum(lp[0:1] * lp[1:2], axis=-1, keepdims=True))
           - jnp.exp(jnp.sum(lp[2:3] * lp[3:4], axis=-1, keepdims=True)) + lam_init)
    o = acc_ref[0] / l_ref[0] - lam * (acc_ref[1] / l_ref[1])
    o_ref[...] = (_rms(o, hg_ref[...]) * (1.0 - lam_init)).astype(o_ref.dtype)


def diff_attention(z, bias, lam_p, head_g, out, row0, nb, S, lam_init):
    tq, tk = bias.shape[2], bias.shape[3]
    assert S % tk == 0 and row0 % S == 0
    nq = S // tq
    rb = row0 // tq
    sb = row0 // S
    vd = DIFF_V_DIM
    q0 = FNET_WIDTH // vd
    k0 = (FNET_WIDTH + DIFF_QK) // vd
    v0 = (FNET_WIDTH + 2 * DIFF_QK) // vd
    o0 = FNET_WIDTH // vd
    return _call_into(
        functools.partial(_diff_attn_kernel, nkc=S // tk, tq=tq, tk=tk, lam_init=lam_init), out,
        (z, z, z, bias, lam_p.astype(F32), head_g.reshape(1, vd).astype(F32)),
        in_specs=[
            pl.BlockSpec((tq, vd), lambda b, h, i: (rb + b * nq + i, q0 + h)),
            pl.BlockSpec((S, vd), lambda b, h, i: (sb + b, k0 + h)),
            pl.BlockSpec((S, vd), lambda b, h, i: (sb + b, v0 + h)),
            pl.BlockSpec((bias.shape[0], None, tq, tk), lambda b, h, i: (0, h, 0, 0)),
            pl.BlockSpec((4, DIFF_HEAD_DIM), lambda b, h, i: (0, 0)),
            pl.BlockSpec((1, vd), lambda b, h, i: (0, 0)),
        ],
        grid=(nb, DIFF_HEADS, nq),
        out_specs=pl.BlockSpec((tq, vd), lambda b, h, i: (rb + b * nq + i, o0 + h)),
        out_shape=jax.ShapeDtypeStruct(out.shape, out.dtype),
        scratch_shapes=[
            pltpu.VMEM((2, tq, 1), F32),
            pltpu.VMEM((2, tq, 1), F32),
            pltpu.VMEM((2, tq, vd), F32),
        ],
        compiler_params=_cparams("parallel", "parallel", "arbitrary"),
        name="diff_attn",
    )


def _log_sigmoid(x):
    return jnp.minimum(x, 0.0) - jnp.log(1.0 + jnp.exp(-jnp.abs(x)))


def _gla_chunk(q, k, v, pre, st, rev):
    C = GLA_CHUNK
    nt = (((1,), (1,)), ((), ()))
    g = _log_sigmoid(pre) * (1.0 / GLA_TAU)
    row = lax.broadcasted_iota(jnp.int32, (C, C), 0)
    col = lax.broadcasted_iota(jnp.int32, (C, C), 1)
    tri = jnp.where((row <= col) if rev else (row >= col), 1.0, 0.0).astype(BF16)
    g_hi = g.astype(BF16)
    g_lo = (g - g_hi.astype(F32)).astype(BF16)
    b = (jnp.dot(tri, g_hi, preferred_element_type=F32)
         + jnp.dot(tri, g_lo, preferred_element_type=F32))
    edge = 0 if rev else C - 1
    b_edge = b[edge:edge + 1, :]

    qe = (q * jnp.exp(b)).astype(BF16)
    inter = lax.dot_general(qe, st.astype(BF16), nt, preferred_element_type=F32)

    sub = GLA_SUB
    lane = lax.broadcasted_iota(jnp.int32, (sub, C), 1)
    trow = lax.broadcasted_iota(jnp.int32, (sub, 1), 0)
    pieces = []
    for j in range(C // sub):
        lo = j * sub
        hi = lo + sub
        qj = q[lo:hi]
        bj = b[lo:hi]
        aj = jnp.zeros((sub, C), F32)
        for s in range(sub):
            ks = k[lo + s:lo + s + 1]
            bs = b[lo + s:lo + s + 1]
            seen = (trow <= s) if rev else (trow >= s)
            e = jnp.exp(jnp.where(seen, bj - bs, -jnp.inf))
            colv = jnp.sum(qj * ks * e, axis=-1, keepdims=True)
            aj = aj + jnp.where(lane == lo + s, colv, 0.0)
        if (hi < C) if rev else (j > 0):
            mj = b[hi:hi + 1] if rev else b[lo - 1:lo]
            qt = (qj * jnp.exp(bj - mj)).astype(BF16)
            kt = (k * jnp.exp(jnp.minimum(mj - b, 0.0))).astype(BF16)
            off = lax.dot_general(qt, kt, nt, preferred_element_type=F32)
            aj = aj + jnp.where((lane >= hi) if rev else (lane < lo), off, 0.0)
        pieces.append(aj)
    attn = jnp.concatenate(pieces, axis=0)
    intra = jnp.dot(attn.astype(BF16), v, preferred_element_type=F32)

    kd = (k * jnp.exp(b_edge - b)).astype(BF16)
    vt = v.astype(F32).T.astype(BF16)
    st_new = st * jnp.exp(b_edge) + jnp.dot(vt, kd, preferred_element_type=F32)
    return inter + intra, st_new


def _gla_kernel(q_ref, k_ref, v_ref, lr_ref, wgu_ref, bg_ref, o_ref, st_ref, *, rows, rev):
    @pl.when(pl.program_id(2) == 0)
    def _():
        st_ref[...] = jnp.zeros_like(st_ref)

    wgu = wgu_ref[...]
    bg = bg_ref[...]
    nc = rows // GLA_CHUNK

    def step(c, carry):
        cc = nc - 1 - c if rev else c
        sl = pl.ds(pl.multiple_of(cc * GLA_CHUNK, GLA_CHUNK), GLA_CHUNK)
        q = q_ref[sl, :].astype(F32) * (GLA_DK ** -0.5)
        k = k_ref[sl, :].astype(F32)
        v = v_ref[sl, :]
        pre = jnp.dot(lr_ref[sl, :].astype(BF16), wgu, preferred_element_type=F32) + bg
        out, st_new = _gla_chunk(q, k, v, pre, st_ref[...], rev)
        o_ref[sl, :] = out
        st_ref[...] = st_new
        return carry

    lax.fori_loop(0, nc, step, 0)


def gla_direction(z, lr, wgu, bg, out, row0, nb, S, rev):
    M = z.shape[0]
    rows = _pick(S, GLA_ROWS, GLA_CHUNK)
    nr = S // rows
    rb = row0 // rows
    kb = GLA_HK // GLA_DK
    vb = 2 * GLA_HK // GLA_DV

    def rblk(b, c):
        return rb + b * nr + (nr - 1 - c if rev else c)

    return _call_into(
        functools.partial(_gla_kernel, rows=rows, rev=rev), out, (z, z, z, lr, wgu, bg),
        in_specs=[
            pl.BlockSpec((rows, GLA_DK), lambda b, h, c: (rblk(b, c), h)),
            pl.BlockSpec((rows, GLA_DK), lambda b, h, c: (rblk(b, c), kb + h)),
            pl.BlockSpec((rows, GLA_DV), lambda b, h, c: (rblk(b, c), vb + h)),
            pl.BlockSpec((rows, GATE_PAD), lambda b, h, c: (rblk(b, c), 0)),
            pl.BlockSpec((GATE_PAD, GLA_DK), lambda b, h, c: (0, h)),
            pl.BlockSpec((1, GLA_DK), lambda b, h, c: (0, h)),
        ],
        grid=(nb, GLA_HEADS, nr),
        out_specs=pl.BlockSpec((rows, GLA_DV), lambda b, h, c: (rblk(b, c), h)),
        out_shape=jax.ShapeDtypeStruct((M, GLA_HV), F32),
        scratch_shapes=[pltpu.VMEM((GLA_DV, GLA_DK), F32)],
        compiler_params=_cparams("parallel", "parallel", "arbitrary"),
        name="gla_rev" if rev else "gla_fwd",
    )


def _gla_post_kernel(of_ref, ob_ref, r_ref, g_ref, o_ref):
    o = _rms(of_ref[...] + ob_ref[...], g_ref[...])
    r = r_ref[...].astype(F32)
    o_ref[...] = (o * (r / (1.0 + jnp.exp(-r)))).astype(o_ref.dtype)


def gla_post(o_f, o_b, z, head_g, tm=512):
    M = o_f.shape[0]
    tm = _pick(M, tm)
    r0 = (2 * GLA_HK + GLA_HV) // GLA_DV
    return pl.pallas_call(
        _gla_post_kernel,
        grid=(M // tm, GLA_HEADS),
        in_specs=[
            pl.BlockSpec((tm, GLA_DV), lambda i, h: (i, h)),
            pl.BlockSpec((tm, GLA_DV), lambda i, h: (i, h)),
            pl.BlockSpec((tm, GLA_DV), lambda i, h: (i, r0 + h)),
            pl.BlockSpec((1, GLA_DV), lambda i, h: (0, 0)),
        ],
        out_specs=pl.BlockSpec((tm, GLA_DV), lambda i, h: (i, h)),
        out_shape=jax.ShapeDtypeStruct((M, GLA_HV), BF16),
        compiler_params=_cparams("parallel", "parallel"),
        name="gla_post",
    )(o_f, o_b, z, head_g.reshape(1, GLA_DV).astype(F32))


def _cross_attn_kernel(q_ref, k_ref, v_ref, o_ref):
    s = lax.dot_general(q_ref[...], k_ref[...], (((1,), (1,)), ((), ())),
                        preferred_element_type=F32) * (X_HEAD_DIM ** -0.5)
    m = jnp.max(s, axis=-1, keepdims=True)
    p = jnp.exp(s - m)
    p = p / jnp.sum(p, axis=-1, keepdims=True)
    o_ref[...] = jnp.dot(p.astype(BF16), v_ref[...], preferred_element_type=F32).astype(o_ref.dtype)


def cross_attention(q, kv, out, row0, nb, S, mem_b0, n_mem, tq=512):
    tq = _pick(S, tq)
    nq = S // tq
    rb = row0 // tq
    hd = X_HEAD_DIM
    return _call_into(
        _cross_attn_kernel, out, (q, kv, kv),
        in_specs=[
            pl.BlockSpec((tq, hd), lambda b, i, h: (rb + b * nq + i, h)),
            pl.BlockSpec((n_mem, hd), lambda b, i, h: (mem_b0 + b, h)),
            pl.BlockSpec((n_mem, hd), lambda b, i, h: (mem_b0 + b, X_HEADS + h)),
        ],
        grid=(nb, nq, X_HEADS),
        out_specs=pl.BlockSpec((tq, hd), lambda b, i, h: (rb + b * nq + i, h)),
        out_shape=jax.ShapeDtypeStruct(q.shape, q.dtype),
        compiler_params=_cparams("parallel", "parallel", "parallel"),
        name="cross_attn",
    )


def _gelu_tanh(x):
    return 0.5 * x * (1.0 + jnp.tanh(math.sqrt(2.0 / math.pi) * (x + 0.044715 * x * x * x)))


def _conv_gate_kernel(ua_ref, uap_ref, uan_ref, ug_ref, ugp_ref, ugn_ref, wa_ref, wg_ref, ba_ref, bg_ref,
                      o_ref, *, nt):
    i = pl.program_id(1)
    first = i == 0
    last = i == nt - 1

    def conv(u_ref, p_ref, n_ref, w_ref, b_ref):
        u = u_ref[...].astype(F32)
        tm = u.shape[0]
        prev = jnp.where(first, 0.0, p_ref[BF16_SUBLANES - 1:BF16_SUBLANES, :].astype(F32))
        nxt = jnp.where(last, 0.0, n_ref[0:1, :].astype(F32))
        rows = lax.broadcasted_iota(jnp.int32, (tm, 1), 0)
        up = jnp.where(rows == 0, prev, pltpu.roll(u, 1, 0))
        un = jnp.where(rows == tm - 1, nxt, pltpu.roll(u, tm - 1, 0))
        w = w_ref[...]
        return w[0:1] * up + w[1:2] * u + w[2:3] * un + b_ref[...]

    a = conv(ua_ref, uap_ref, uan_ref, wa_ref, ba_ref)
    g = conv(ug_ref, ugp_ref, ugn_ref, wg_ref, bg_ref)
    o_ref[...] = (_gelu_tanh(g) * a).astype(o_ref.dtype)


def conv_gate(u, conv_w, conv_b, out, row0, nb, S, tm=512, tn=512):
    M, F2 = u.shape
    F = F2 // 2
    tm = _pick(S, tm, BF16_SUBLANES)
    tn = _pick(F, tn, 128)
    nt = S // tm
    nj = F // tn
    rb = row0 // tm
    hb = tm // BF16_SUBLANES
    h_last = M // BF16_SUBLANES - 1

    def main(off):
        return pl.BlockSpec((tm, tn), lambda b, i, j: (rb + b * nt + i, off + j))

    def prev(off):
        return pl.BlockSpec((BF16_SUBLANES, tn),
                            lambda b, i, j: (jnp.maximum((rb + b * nt + i) * hb - 1, 0), off + j))

    def nxt(off):
        return pl.BlockSpec((BF16_SUBLANES, tn),
                            lambda b, i, j: (jnp.minimum((rb + b * nt + i + 1) * hb, h_last), off + j))

    def cols(rows, off):
        return pl.BlockSpec((rows, tn), lambda b, i, j: (0, off + j))

    return _call_into(
        functools.partial(_conv_gate_kernel, nt=nt), out,
        (u, u, u, u, u, u, conv_w, conv_w, conv_b.reshape(1, F2), conv_b.reshape(1, F2)),
        in_specs=[main(0), prev(0), nxt(0), main(nj), prev(nj), nxt(nj),
                  cols(3, 0), cols(3, nj), cols(1, 0), cols(1, nj)],
        grid=(nb, nt, nj),
        out_specs=pl.BlockSpec((tm, tn), lambda b, i, j: (rb + b * nt + i, j)),
        out_shape=jax.ShapeDtypeStruct((M, F), BF16),
        compiler_params=_cparams("parallel", "parallel", "parallel"),
        name="conv_gate",
    )


def kernel(x_prompt, x_sample, mem_prompt, mem_sample, norm_g, rel_bias, w_in_even, w_out_even,
           diff_lambda, diff_norm_g, w_in_odd, gla_gate_down, gla_gate_up, gla_gate_bias, gla_norm_g,
           w_out_odd, w_xq, w_xkv, w_xo, w_up, conv_w, conv_b, w_down):
    bp, sp, D = x_prompt.shape
    bs, ss, _ = x_sample.shape
    n_mem = mem_prompt.shape[1]
    segs = ((0, bp, sp), (bp * sp, bs, ss))
    M = bp * sp + bs * ss
    x = jnp.concatenate([x_prompt.reshape(bp * sp, D), x_sample.reshape(bs * ss, D)], axis=0)
    mem = jnp.concatenate([mem_prompt.reshape(bp * n_mem, D), mem_sample.reshape(bs * n_mem, D)], axis=0)
    mem_b0 = (0, bp)
    norm_g = norm_g.astype(F32)

    tk = min(DIFF_TK, sp, ss)
    bias = bias_tiles(rel_bias, min(DIFF_TQ, tk), tk)
    gd = FNET_GROUP_DIM
    c_ch, s_ch = _dft_tables(gd, gd ** -0.5)
    w_ch = jnp.concatenate([c_ch, -s_ch], axis=1).astype(BF16)
    fcat = {}
    for _, _, S in segs:
        if S not in fcat:
            c_s, s_s = _dft_tables(S, S ** -0.5)
            fcat[S] = jnp.concatenate([c_s, s_s], axis=1).astype(BF16)
    q_cols = (jnp.arange(w_in_even.shape[2]) >= FNET_WIDTH) & (jnp.arange(w_in_even.shape[2]) < FNET_WIDTH + DIFF_QK)
    col_scale = jnp.where(q_cols, DIFF_HEAD_DIM ** -0.5 * LOG2E, 1.0).astype(F32)

    for l in range(DEPTH):
        ng = norm_g[l]
        i = l // 2
        if l % 2 == 0:
            lam_init = 0.8 - 0.6 * math.exp(-0.3 * l)
            z = prenorm_matmul(x, ng[0], (w_in_even[i] * col_scale).astype(BF16), BF16)
            y = fnet_channel_dft(z, w_ch)
            mixed = None
            mixed_shape = jax.ShapeDtypeStruct((M, FNET_WIDTH + DIFF_WIDTH), BF16)
            for row0, nb, S in segs:
                mixed = fnet_seq_dft(fcat[S], y, mixed, mixed_shape, row0, nb, S)
            for row0, nb, S in segs:
                mixed = diff_attention(z, bias, diff_lambda[i], diff_norm_g[i], mixed, row0, nb, S, lam_init)
            x = matmul_postnorm_residual(mixed, w_out_even[i].astype(BF16), ng[1], x)
        else:
            z = prenorm_matmul(x, ng[0], w_in_odd[i].astype(BF16), BF16)
            w_gd = jnp.concatenate([gla_gate_down[i, 0], gla_gate_down[i, 1]], axis=1)
            w_gd = jnp.pad(w_gd, ((0, 0), (0, GATE_PAD - 2 * GLA_RANK))).astype(BF16)
            lr = prenorm_matmul(x, ng[0], w_gd, F32)
            outs = []
            for d in range(2):
                wgu = jnp.zeros((GATE_PAD, GLA_HK), F32)
                wgu = wgu.at[d * GLA_RANK:(d + 1) * GLA_RANK].set(gla_gate_up[i, d]).astype(BF16)
                bg = gla_gate_bias[i, d].reshape(1, GLA_HK).astype(F32)
                o = None
                for row0, nb, S in segs:
                    o = gla_direction(z, lr, wgu, bg, o, row0, nb, S, rev=(d == 1))
                outs.append(o)
            mixed = gla_post(outs[0], outs[1], z, gla_norm_g[i])
            x = matmul_postnorm_residual(mixed, w_out_odd[i].astype(BF16), ng[1], x)

        q = prenorm_matmul(x, ng[2], w_xq[l].astype(BF16), BF16)
        kv = prenorm_matmul(mem, ng[4], w_xkv[l].astype(BF16), BF16)
        att = None
        for (row0, nb, S), b0 in zip(segs, mem_b0):
            att = cross_attention(q, kv, att, row0, nb, S, b0, n_mem)
        x = matmul_postnorm_residual(att, w_xo[l].astype(BF16), ng[3], x)

        u = prenorm_matmul(x, ng[5], w_up[l].astype(BF16), BF16)
        act = None
        for row0, nb, S in segs:
            act = conv_gate(u, conv_w[l].astype(F32), conv_b[l].astype(F32), act, row0, nb, S)
        x = matmul_postnorm_residual(act, w_down[l].astype(BF16), ng[6], x)

    y_prompt = x[:bp * sp].reshape(bp, sp, D)
    y_sample = x[bp * sp:].reshape(bs, ss, D)
    return (y_prompt, y_sample)
```

```python
import functools
import math

import jax
import jax.numpy as jnp
from jax import lax
from jax.experimental import pallas as pl
from jax.experimental.pallas import tpu as pltpu

F32 = jnp.float32
BF16 = jnp.bfloat16

EPS = 1e-6
DEPTH = 4
LOG2E = math.log2(math.e)
FNET_GROUPS = 4
FNET_GROUP_DIM = 256
FNET_WIDTH = FNET_GROUPS * FNET_GROUP_DIM
DIFF_HEADS = 8
DIFF_HEAD_DIM = 64
DIFF_V_DIM = 2 * DIFF_HEAD_DIM
DIFF_QK = DIFF_HEADS * 2 * DIFF_HEAD_DIM
DIFF_WIDTH = DIFF_HEADS * DIFF_V_DIM
N_BUCKETS = 32
MAX_DISTANCE = 128
DIFF_TQ = 256
DIFF_TK = 1024
GLA_HEADS = 4
GLA_DK = 256
GLA_DV = 512
GLA_RANK = 16
GLA_TAU = 16.0
GLA_HK = GLA_HEADS * GLA_DK
GLA_HV = GLA_HEADS * GLA_DV
GLA_CHUNK = 64
GLA_SUB = 16
GLA_ROWS = 256
GATE_PAD = 128
X_HEADS = 4
X_HEAD_DIM = 512
VMEM_LIMIT_BYTES = 52 * 1024 * 1024
BF16_SUBLANES = 16


def _cparams(*sem):
    return pltpu.CompilerParams(dimension_semantics=sem, vmem_limit_bytes=VMEM_LIMIT_BYTES)


def _pick(n, pref, mult=8):
    if n <= pref:
        return n
    for t in range(pref, 0, -1):
        if n % t == 0 and t % mult == 0:
            return t
    return n


def _rms(x, g):
    return x * lax.rsqrt(jnp.mean(x * x, axis=-1, keepdims=True) + EPS) * g


def _call_into(body, out, args, in_specs, **kw):
    if out is None:
        return pl.pallas_call(body, in_specs=in_specs, **kw)(*args)
    n = len(args)

    def aliased(*refs):
        body(*refs[:n], *refs[n + 1:])

    return pl.pallas_call(aliased, in_specs=list(in_specs) + [pl.BlockSpec(memory_space=pl.ANY)],
                          input_output_aliases={n: 0}, **kw)(*args, out)


def _prenorm_mm_kernel(x_ref, g_ref, w_ref, o_ref, xn_ref):
    @pl.when(pl.program_id(1) == 0)
    def _():
        xn_ref[...] = _rms(x_ref[...], g_ref[...]).astype(BF16)

    o_ref[...] = jnp.dot(xn_ref[...], w_ref[...], preferred_element_type=F32).astype(o_ref.dtype)


def prenorm_matmul(x, g, w, out_dtype, tm=1024, tn=1024):
    M, K = x.shape
    N = w.shape[1]
    tm = _pick(M, tm)
    tn = _pick(N, tn, 128)
    return pl.pallas_call(
        _prenorm_mm_kernel,
        grid=(M // tm, N // tn),
        in_specs=[
            pl.BlockSpec((tm, K), lambda i, j: (i, 0)),
            pl.BlockSpec((1, K), lambda i, j: (0, 0)),
            pl.BlockSpec((K, tn), lambda i, j: (0, j)),
        ],
        out_specs=pl.BlockSpec((tm, tn), lambda i, j: (i, j)),
        out_shape=jax.ShapeDtypeStruct((M, N), out_dtype),
        scratch_shapes=[pltpu.VMEM((tm, K), BF16)],
        compiler_params=_cparams("parallel", "arbitrary"),
        name="prenorm_mm",
    )(x, g.reshape(1, K), w)


def _mm_postnorm_res_kernel(a_ref, w_ref, g_ref, x_ref, o_ref, acc_ref, *, nk):
    k = pl.program_id(1)

    @pl.when(k == 0)
    def _():
        acc_ref[...] = jnp.zeros_like(acc_ref)

    acc_ref[...] += jnp.dot(a_ref[...], w_ref[...], preferred_element_type=F32)

    @pl.when(k == nk - 1)
    def _():
        o_ref[...] = x_ref[...] + _rms(acc_ref[...], g_ref[...])


def _mm_postnorm_res_fullk_kernel(a_ref, w_ref, g_ref, x_ref, o_ref):
    y = jnp.dot(a_ref[...], w_ref[...], preferred_element_type=F32)
    o_ref[...] = x_ref[...] + _rms(y, g_ref[...])


def matmul_postnorm_residual(a, w, g, x, tm=512, tk=2048):
    M, K = a.shape
    D = w.shape[1]
    tm = _pick(M, tm)
    tk = _pick(K, tk, 128)
    nk = K // tk
    if nk == 1:
        return pl.pallas_call(
            _mm_postnorm_res_fullk_kernel,
            grid=(M // tm,),
            in_specs=[
                pl.BlockSpec((tm, K), lambda i: (i, 0)),
                pl.BlockSpec((K, D), lambda i: (0, 0)),
                pl.BlockSpec((1, D), lambda i: (0, 0)),
                pl.BlockSpec((tm, D), lambda i: (i, 0)),
            ],
            out_specs=pl.BlockSpec((tm, D), lambda i: (i, 0)),
            out_shape=jax.ShapeDtypeStruct((M, D), F32),
            compiler_params=_cparams("parallel"),
            name="mm_postnorm_res_fullk",
        )(a, w, g.reshape(1, D), x)
    return pl.pallas_call(
        functools.partial(_mm_postnorm_res_kernel, nk=nk),
        grid=(M // tm, nk),
        in_specs=[
            pl.BlockSpec((tm, tk), lambda i, k: (i, k)),
            pl.BlockSpec((tk, D), lambda i, k: (k, 0)),
            pl.BlockSpec((1, D), lambda i, k: (0, 0)),
            pl.BlockSpec((tm, D), lambda i, k: (i, 0)),
        ],
        out_specs=pl.BlockSpec((tm, D), lambda i, k: (i, 0)),
        out_shape=jax.ShapeDtypeStruct((M, D), F32),
        scratch_shapes=[pltpu.VMEM((tm, D), F32)],
        compiler_params=_cparams("parallel", "arbitrary"),
        name="mm_postnorm_res",
    )(a, w, g.reshape(1, D), x)


def _dft_tables(n, scale):
    split = 1
    while split * split < n:
        split *= 2
    hi = n // split
    k = jnp.arange(n, dtype=jnp.int32)[None, :]
    a = jnp.arange(hi, dtype=jnp.int32)[:, None] * split
    b = jnp.arange(split, dtype=jnp.int32)[:, None]
    ang_a = ((a * k) % n).astype(F32) * (2.0 * math.pi / n)
    ang_b = ((b * k) % n).astype(F32) * (2.0 * math.pi / n)
    ca, sa = jnp.cos(ang_a)[:, None, :], jnp.sin(ang_a)[:, None, :]
    cb, sb = jnp.cos(ang_b)[None, :, :], jnp.sin(ang_b)[None, :, :]
    c = (ca * cb - sa * sb) * scale
    s = (sa * cb + ca * sb) * scale
    return c.reshape(n, n), s.reshape(n, n)


def _fnet_channel_kernel(u_ref, w_ref, o_ref):
    y = jnp.dot(u_ref[...], w_ref[...], preferred_element_type=F32)
    o_ref[0] = y[:, :FNET_GROUP_DIM].astype(BF16)
    o_ref[1] = y[:, FNET_GROUP_DIM:].astype(BF16)


def fnet_channel_dft(z, w_ch, tm=1024):
    M = z.shape[0]
    tm = _pick(M, tm)
    gd = FNET_GROUP_DIM
    return pl.pallas_call(
        _fnet_channel_kernel,
        grid=(M // tm, FNET_GROUPS),
        in_specs=[
            pl.BlockSpec((tm, gd), lambda i, g: (i, g)),
            pl.BlockSpec((gd, 2 * gd), lambda i, g: (0, 0)),
        ],
        out_specs=pl.BlockSpec((2, tm, gd), lambda i, g: (0, i, g)),
        out_shape=jax.ShapeDtypeStruct((2, M, FNET_WIDTH), BF16),
        compiler_params=_cparams("parallel", "parallel"),
        name="fnet_channel",
    )(z, w_ch)


def _fnet_seq_kernel(f_ref, y_ref, o_ref, acc_ref, *, nk):
    k = pl.program_id(2)

    @pl.when(k == 0)
    def _():
        acc_ref[...] = jnp.zeros_like(acc_ref)

    acc_ref[...] += jnp.dot(f_ref[...], y_ref[...], preferred_element_type=F32)

    @pl.when(k == nk - 1)
    def _():
        o_ref[...] = acc_ref[...].astype(o_ref.dtype)


def fnet_seq_dft(fcat, y, out, out_shape, row0, nb, S, tm=1024, tk=512):
    tm = _pick(S, tm)
    tk = _pick(S, tk, 128)
    nks = S // tk
    nk = 2 * nks
    C = y.shape[2]
    rb_k = row0 // tk
    rb_m = row0 // tm
    return _call_into(
        functools.partial(_fnet_seq_kernel, nk=nk), out, (fcat, y),
        in_specs=[
            pl.BlockSpec((tm, tk), lambda b, i, k: (i, k)),
            pl.BlockSpec((None, tk, C), lambda b, i, k: (k // nks, rb_k + b * nks + k % nks, 0)),
        ],
        grid=(nb, S // tm, nk),
        out_specs=pl.BlockSpec((tm, C), lambda b, i, k: (rb_m + b * (S // tm) + i, 0)),
        out_shape=out_shape,
        scratch_shapes=[pltpu.VMEM((tm, C), F32)],
        compiler_params=_cparams("parallel", "parallel", "arbitrary"),
        name="fnet_seq",
    )


def _t5_bucket(rel):
    n = N_BUCKETS // 2
    max_exact = n // 2
    base = jnp.where(rel > 0, n, 0)
    a = jnp.abs(rel)
    af = jnp.maximum(a, 1).astype(F32)
    large = max_exact + (jnp.log(af / max_exact) / math.log(MAX_DISTANCE / max_exact)
                         * (n - max_exact)).astype(jnp.int32)
    large = jnp.minimum(large, n - 1)
    return base + jnp.where(a < max_exact, a, large)


def _bias_tile_kernel(rb_ref, bucket_ref, o_ref):
    h = pl.program_id(1)
    bucket = bucket_ref[...]
    acc = jnp.zeros(bucket.shape, F32)
    for n in range(N_BUCKETS):
        acc = jnp.where(bucket == n, rb_ref[n, h], acc)
    o_ref[...] = acc * LOG2E


def bias_tiles(rel_bias, tq, tk):
    r = tk // tq
    assert tq >= MAX_DISTANCE and tk % tq == 0
    n = 2 * r + 2
    d = jnp.arange(-r - 1, r + 1, dtype=jnp.int32)[:, None, None] * tq
    qq = jnp.arange(tq, dtype=jnp.int32)[None, :, None]
    kk = jnp.arange(tk, dtype=jnp.int32)[None, None, :]
    bucket = _t5_bucket(d + kk - qq)
    return pl.pallas_call(
        _bias_tile_kernel,
        grid=(n, DIFF_HEADS),
        in_specs=[
            pl.BlockSpec(memory_space=pltpu.SMEM),
            pl.BlockSpec((None, tq, tk), lambda t, h: (t, 0, 0)),
        ],
        out_specs=pl.BlockSpec((None, None, tq, tk), lambda t, h: (t, h, 0, 0)),
        out_shape=jax.ShapeDtypeStruct((n, DIFF_HEADS, tq, tk), F32),
        compiler_params=_cparams("parallel", "parallel"),
        name="bias_tiles",
    )(rel_bias.astype(F32), bucket)


def _diff_attn_kernel(q_ref, k_ref, v_ref, bias_ref, lam_ref, hg_ref, o_ref, m_ref, acc_ref,
                      *, nkc, tq, tk, lam_init):
    i = pl.program_id(2)
    r = tk // tq
    vd = DIFF_V_DIM
    m_ref[...] = jnp.full_like(m_ref, -jnp.inf)
    acc_ref[...] = jnp.zeros_like(acc_ref)
    ones = jnp.ones((tk, vd), BF16)

    q = q_ref[...]
    lane = lax.broadcasted_iota(jnp.int32, q.shape, 1)
    zero = jnp.zeros_like(q)
    halves = (jnp.where(lane < DIFF_HEAD_DIM, q, zero), jnp.where(lane >= DIFF_HEAD_DIM, q, zero))

    def chunk(j, carry):
        rows = pl.ds(pl.multiple_of(j * tk, tk), tk)
        k = k_ref[rows, :]
        vx = jnp.concatenate([v_ref[rows, :], ones], axis=1)
        bias = bias_ref[jnp.clip(j * r - i, -r - 1, r) + r + 1]
        for c, qc in enumerate(halves):
            s = lax.dot_general(qc, k, (((1,), (1,)), ((), ())), preferred_element_type=F32) + bias
            m_prev = m_ref[c]
            m_new = jnp.maximum(m_prev, jnp.max(s, axis=-1, keepdims=True))
            alpha = jnp.exp2(m_prev - m_new)
            p = jnp.exp2(s - m_new).astype(BF16)
            acc_ref[c] = alpha * acc_ref[c] + jnp.dot(p, vx, preferred_element_type=F32)
            m_ref[c] = m_new
        return carry

    lax.fori_loop(0, nkc, chunk, 0)

    lp = lam_ref[...]
    lam = (jnp.exp(jnp.sum(lp[0:1] * lp[1:2], axis=-1, keepdims=True))
           - jnp.exp(jnp.sum(lp[2:3] * lp[3:4], axis=-1, keepdims=True)) + lam_init)
    a0 = acc_ref[0]
    a1 = acc_ref[1]
    o = a0[:, :vd] / a0[:, vd:] - lam * (a1[:, :vd] / a1[:, vd:])
    o_ref[...] = (_rms(o, hg_ref[...]) * (1.0 - lam_init)).astype(o_ref.dtype)


def diff_attention(z, bias, lam_p, head_g, out, row0, nb, S, lam_init):
    tq, tk = bias.shape[2], bias.shape[3]
    assert S % tk == 0 and row0 % S == 0
    nq = S // tq
    rb = row0 // tq
    sb = row0 // S
    vd = DIFF_V_DIM
    q0 = FNET_WIDTH // vd
    k0 = (FNET_WIDTH + DIFF_QK) // vd
    v0 = (FNET_WIDTH + 2 * DIFF_QK) // vd
    o0 = FNET_WIDTH // vd
    return _call_into(
        functools.partial(_diff_attn_kernel, nkc=S // tk, tq=tq, tk=tk, lam_init=lam_init), out,
        (z, z, z, bias, lam_p.astype(F32), head_g.reshape(1, vd).astype(F32)),
        in_specs=[
            pl.BlockSpec((tq, vd), lambda b, h, i: (rb + b * nq + i, q0 + h)),
            pl.BlockSpec((S, vd), lambda b, h, i: (sb + b, k0 + h)),
            pl.BlockSpec((S, vd), lambda b, h, i: (sb + b, v0 + h)),
            pl.BlockSpec((bias.shape[0], None, tq, tk), lambda b, h, i: (0, h, 0, 0)),
            pl.BlockSpec((4, DIFF_HEAD_DIM), lambda b, h, i: (0, 0)),
            pl.BlockSpec((1, vd), lambda b, h, i: (0, 0)),
        ],
        grid=(nb, DIFF_HEADS, nq),
        out_specs=pl.BlockSpec((tq, vd), lambda b, h, i: (rb + b * nq + i, o0 + h)),
        out_shape=jax.ShapeDtypeStruct(out.shape, out.dtype),
        scratch_shapes=[
            pltpu.VMEM((2, tq, 1), F32),
            pltpu.VMEM((2, tq, 2 * vd), F32),
        ],
        compiler_params=_cparams("parallel", "parallel", "arbitrary"),
        name="diff_attn",
    )


def _log_sigmoid(x):
    return jnp.minimum(x, 0.0) - jnp.log(1.0 + jnp.exp(-jnp.abs(x)))


def _gla_chunk(q, k, v, pre, st, rev):
    C = GLA_CHUNK
    nt = (((1,), (1,)), ((), ()))
    g = _log_sigmoid(pre) * (1.0 / GLA_TAU)
    row = lax.broadcasted_iota(jnp.int32, (C, C), 0)
    col = lax.broadcasted_iota(jnp.int32, (C, C), 1)
    tri = jnp.where((row <= col) if rev else (row >= col), 1.0, 0.0).astype(BF16)
    g_hi = g.astype(BF16)
    g_lo = (g - g_hi.astype(F32)).astype(BF16)
    b = (jnp.dot(tri, g_hi, preferred_element_type=F32)
         + jnp.dot(tri, g_lo, preferred_element_type=F32))
    edge = 0 if rev else C - 1
    b_edge = b[edge:edge + 1, :]

    qe = (q * jnp.exp(b)).astype(BF16)
    inter = lax.dot_general(qe, st.astype(BF16), nt, preferred_element_type=F32)

    sub = GLA_SUB
    lane = lax.broadcasted_iota(jnp.int32, (sub, C), 1)
    trow = lax.broadcasted_iota(jnp.int32, (sub, 1), 0)
    pieces = []
    for j in range(C // sub):
        lo = j * sub
        hi = lo + sub
        qj = q[lo:hi]
        bj = b[lo:hi]
        aj = jnp.zeros((sub, C), F32)
        for s in range(sub):
            ks = k[lo + s:lo + s + 1]
            bs = b[lo + s:lo + s + 1]
            seen = (trow <= s) if rev else (trow >= s)
            e = jnp.exp(jnp.where(seen, bj - bs, -jnp.inf))
            colv = jnp.sum(qj * ks * e, axis=-1, keepdims=True)
            aj = aj + jnp.where(lane == lo + s, colv, 0.0)
        if (hi < C) if rev else (j > 0):
            mj = b[hi:hi + 1] if rev else b[lo - 1:lo]
            qt = (qj * jnp.exp(bj - mj)).astype(BF16)
            kt = (k * jnp.exp(jnp.minimum(mj - b, 0.0))).astype(BF16)
            off = lax.dot_general(qt, kt, nt, preferred_element_type=F32)
            aj = aj + jnp.where((lane >= hi) if rev else (lane < lo), off, 0.0)
        pieces.append(aj)
    attn = jnp.concatenate(pieces, axis=0)
    intra = jnp.dot(attn.astype(BF16), v, preferred_element_type=F32)

    kd = (k * jnp.exp(b_edge - b)).astype(BF16)
    vt = v.astype(F32).T.astype(BF16)
    st_new = st * jnp.exp(b_edge) + jnp.dot(vt, kd, preferred_element_type=F32)
    return inter + intra, st_new


def _gla_kernel(q_ref, k_ref, v_ref, lr_ref, wgu_ref, bg_ref, o_ref, st_ref, *, rows, rev):
    @pl.when(pl.program_id(2) == 0)
    def _():
        st_ref[...] = jnp.zeros_like(st_ref)

    wgu = wgu_ref[...]
    bg = bg_ref[...]
    nc = rows // GLA_CHUNK

    def step(c, carry):
        cc = nc - 1 - c if rev else c
        sl = pl.ds(pl.multiple_of(cc * GLA_CHUNK, GLA_CHUNK), GLA_CHUNK)
        q = q_ref[sl, :].astype(F32) * (GLA_DK ** -0.5)
        k = k_ref[sl, :].astype(F32)
        v = v_ref[sl, :]
        pre = jnp.dot(lr_ref[sl, :].astype(BF16), wgu, preferred_element_type=F32) + bg
        out, st_new = _gla_chunk(q, k, v, pre, st_ref[...], rev)
        o_ref[sl, :] = out
        st_ref[...] = st_new
        return carry

    lax.fori_loop(0, nc, step, 0)


def gla_direction(z, lr, wgu, bg, out, row0, nb, S, rev):
    M = z.shape[0]
    rows = _pick(S, GLA_ROWS, GLA_CHUNK)
    nr = S // rows
    rb = row0 // rows
    kb = GLA_HK // GLA_DK
    vb = 2 * GLA_HK // GLA_DV

    def rblk(b, c):
        return rb + b * nr + (nr - 1 - c if rev else c)

    return _call_into(
        functools.partial(_gla_kernel, rows=rows, rev=rev), out, (z, z, z, lr, wgu, bg),
        in_specs=[
            pl.BlockSpec((rows, GLA_DK), lambda b, h, c: (rblk(b, c), h)),
            pl.BlockSpec((rows, GLA_DK), lambda b, h, c: (rblk(b, c), kb + h)),
            pl.BlockSpec((rows, GLA_DV), lambda b, h, c: (rblk(b, c), vb + h)),
            pl.BlockSpec((rows, GATE_PAD), lambda b, h, c: (rblk(b, c), 0)),
            pl.BlockSpec((GATE_PAD, GLA_DK), lambda b, h, c: (0, h)),
            pl.BlockSpec((1, GLA_DK), lambda b, h, c: (0, h)),
        ],
        grid=(nb, GLA_HEADS, nr),
        out_specs=pl.BlockSpec((rows, GLA_DV), lambda b, h, c: (rblk(b, c), h)),
        out_shape=jax.ShapeDtypeStruct((M, GLA_HV), F32),
        scratch_shapes=[pltpu.VMEM((GLA_DV, GLA_DK), F32)],
        compiler_params=_cparams("parallel", "parallel", "arbitrary"),
        name="gla_rev" if rev else "gla_fwd",
    )


def _gla_post_kernel(of_ref, ob_ref, r_ref, g_ref, o_ref):
    o = _rms(of_ref[...] + ob_ref[...], g_ref[...])
    r = r_ref[...].astype(F32)
    o_ref[...] = (o * (r / (1.0 + jnp.exp(-r)))).astype(o_ref.dtype)


def gla_post(o_f, o_b, z, head_g, tm=512):
    M = o_f.shape[0]
    tm = _pick(M, tm)
    r0 = (2 * GLA_HK + GLA_HV) // GLA_DV
    return pl.pallas_call(
        _gla_post_kernel,
        grid=(M // tm, GLA_HEADS),
        in_specs=[
            pl.BlockSpec((tm, GLA_DV), lambda i, h: (i, h)),
            pl.BlockSpec((tm, GLA_DV), lambda i, h: (i, h)),
            pl.BlockSpec((tm, GLA_DV), lambda i, h: (i, r0 + h)),
            pl.BlockSpec((1, GLA_DV), lambda i, h: (0, 0)),
        ],
        out_specs=pl.BlockSpec((tm, GLA_DV), lambda i, h: (i, h)),
        out_shape=jax.ShapeDtypeStruct((M, GLA_HV), BF16),
        compiler_params=_cparams("parallel", "parallel"),
        name="gla_post",
    )(o_f, o_b, z, head_g.reshape(1, GLA_DV).astype(F32))


def _cross_attn_kernel(q_ref, k_ref, v_ref, o_ref):
    s = lax.dot_general(q_ref[...], k_ref[...], (((1,), (1,)), ((), ())),
                        preferred_element_type=F32) * (X_HEAD_DIM ** -0.5)
    m = jnp.max(s, axis=-1, keepdims=True)
    p = jnp.exp(s - m)
    p = p / jnp.sum(p, axis=-1, keepdims=True)
    o_ref[...] = jnp.dot(p.astype(BF16), v_ref[...], preferred_element_type=F32).astype(o_ref.dtype)


def cross_attention(q, kv, out, row0, nb, S, mem_b0, n_mem, tq=512):
    tq = _pick(S, tq)
    nq = S // tq
    rb = row0 // tq
    hd = X_HEAD_DIM
    return _call_into(
        _cross_attn_kernel, out, (q, kv, kv),
        in_specs=[
            pl.BlockSpec((tq, hd), lambda b, i, h: (rb + b * nq + i, h)),
            pl.BlockSpec((n_mem, hd), lambda b, i, h: (mem_b0 + b, h)),
            pl.BlockSpec((n_mem, hd), lambda b, i, h: (mem_b0 + b, X_HEADS + h)),
        ],
        grid=(nb, nq, X_HEADS),
        out_specs=pl.BlockSpec((tq, hd), lambda b, i, h: (rb + b * nq + i, h)),
        out_shape=jax.ShapeDtypeStruct(q.shape, q.dtype),
        compiler_params=_cparams("parallel", "parallel", "parallel"),
        name="cross_attn",
    )


def _gelu_tanh(x):
    return 0.5 * x * (1.0 + jnp.tanh(math.sqrt(2.0 / math.pi) * (x + 0.044715 * x * x * x)))


def _conv_gate_kernel(ua_ref, uap_ref, uan_ref, ug_ref, ugp_ref, ugn_ref, wa_ref, wg_ref, ba_ref, bg_ref,
                      o_ref, *, nt):
    i = pl.program_id(1)
    first = i == 0
    last = i == nt - 1

    def conv(u_ref, p_ref, n_ref, w_ref, b_ref):
        u = u_ref[...].astype(F32)
        tm = u.shape[0]
        prev = jnp.where(first, 0.0, p_ref[BF16_SUBLANES - 1:BF16_SUBLANES, :].astype(F32))
        nxt = jnp.where(last, 0.0, n_ref[0:1, :].astype(F32))
        rows = lax.broadcasted_iota(jnp.int32, (tm, 1), 0)
        up = jnp.where(rows == 0, prev, pltpu.roll(u, 1, 0))
        un = jnp.where(rows == tm - 1, nxt, pltpu.roll(u, tm - 1, 0))
        w = w_ref[...]
        return w[0:1] * up + w[1:2] * u + w[2:3] * un + b_ref[...]

    a = conv(ua_ref, uap_ref, uan_ref, wa_ref, ba_ref)
    g = conv(ug_ref, ugp_ref, ugn_ref, wg_ref, bg_ref)
    o_ref[...] = (_gelu_tanh(g) * a).astype(o_ref.dtype)


def conv_gate(u, conv_w, conv_b, out, row0, nb, S, tm=512, tn=512):
    M, F2 = u.shape
    F = F2 // 2
    tm = _pick(S, tm, BF16_SUBLANES)
    tn = _pick(F, tn, 128)
    nt = S // tm
    nj = F // tn
    rb = row0 // tm
    hb = tm // BF16_SUBLANES
    h_last = M // BF16_SUBLANES - 1

    def main(off):
        return pl.BlockSpec((tm, tn), lambda b, i, j: (rb + b * nt + i, off + j))

    def prev(off):
        return pl.BlockSpec((BF16_SUBLANES, tn),
                            lambda b, i, j: (jnp.maximum((rb + b * nt + i) * hb - 1, 0), off + j))

    def nxt(off):
        return pl.BlockSpec((BF16_SUBLANES, tn),
                            lambda b, i, j: (jnp.minimum((rb + b * nt + i + 1) * hb, h_last), off + j))

    def cols(rows, off):
        return pl.BlockSpec((rows, tn), lambda b, i, j: (0, off + j))

    return _call_into(
        functools.partial(_conv_gate_kernel, nt=nt), out,
        (u, u, u, u, u, u, conv_w, conv_w, conv_b.reshape(1, F2), conv_b.reshape(1, F2)),
        in_specs=[main(0), prev(0), nxt(0), main(nj), prev(nj), nxt(nj),
                  cols(3, 0), cols(3, nj), cols(1, 0), cols(1, nj)],
        grid=(nb, nt, nj),
        out_specs=pl.BlockSpec((tm, tn), lambda b, i, j: (rb + b * nt + i, j)),
        out_shape=jax.ShapeDtypeStruct((M, F), BF16),
        compiler_params=_cparams("parallel", "parallel", "parallel"),
        name="conv_gate",
    )


def kernel(x_prompt, x_sample, mem_prompt, mem_sample, norm_g, rel_bias, w_in_even, w_out_even,
           diff_lambda, diff_norm_g, w_in_odd, gla_gate_down, gla_gate_up, gla_gate_bias, gla_norm_g,
           w_out_odd, w_xq, w_xkv, w_xo, w_up, conv_w, conv_b, w_down):
    bp, sp, D = x_prompt.shape
    bs, ss, _ = x_sample.shape
    n_mem = mem_prompt.shape[1]
    segs = ((0, bp, sp), (bp * sp, bs, ss))
    M = bp * sp + bs * ss
    x = jnp.concatenate([x_prompt.reshape(bp * sp, D), x_sample.reshape(bs * ss, D)], axis=0)
    mem = jnp.concatenate([mem_prompt.reshape(bp * n_mem, D), mem_sample.reshape(bs * n_mem, D)], axis=0)
    mem_b0 = (0, bp)
    norm_g = norm_g.astype(F32)

    tk = min(DIFF_TK, sp, ss)
    bias = bias_tiles(rel_bias, min(DIFF_TQ, tk), tk)
    gd = FNET_GROUP_DIM
    c_ch, s_ch = _dft_tables(gd, gd ** -0.5)
    w_ch = jnp.concatenate([c_ch, -s_ch], axis=1).astype(BF16)
    fcat = {}
    for _, _, S in segs:
        if S not in fcat:
            c_s, s_s = _dft_tables(S, S ** -0.5)
            fcat[S] = jnp.concatenate([c_s, s_s], axis=1).astype(BF16)
    q_cols = (jnp.arange(w_in_even.shape[2]) >= FNET_WIDTH) & (jnp.arange(w_in_even.shape[2]) < FNET_WIDTH + DIFF_QK)
    col_scale = jnp.where(q_cols, DIFF_HEAD_DIM ** -0.5 * LOG2E, 1.0).astype(F32)

    for l in range(DEPTH):
        ng = norm_g[l]
        i = l // 2
        if l % 2 == 0:
            lam_init = 0.8 - 0.6 * math.exp(-0.3 * l)
            z = prenorm_matmul(x, ng[0], (w_in_even[i] * col_scale).astype(BF16), BF16)
            y = fnet_channel_dft(z, w_ch)
            mixed = None
            mixed_shape = jax.ShapeDtypeStruct((M, FNET_WIDTH + DIFF_WIDTH), BF16)
            for row0, nb, S in segs:
                mixed = fnet_seq_dft(fcat[S], y, mixed, mixed_shape, row0, nb, S)
            for row0, nb, S in segs:
                mixed = diff_attention(z, bias, diff_lambda[i], diff_norm_g[i], mixed, row0, nb, S, lam_init)
            x = matmul_postnorm_residual(mixed, w_out_even[i].astype(BF16), ng[1], x)
        else:
            z = prenorm_matmul(x, ng[0], w_in_odd[i].astype(BF16), BF16)
            w_gd = jnp.concatenate([gla_gate_down[i, 0], gla_gate_down[i, 1]], axis=1)
            w_gd = jnp.pad(w_gd, ((0, 0), (0, GATE_PAD - 2 * GLA_RANK))).astype(BF16)
            lr = prenorm_matmul(x, ng[0], w_gd, F32)
            outs = []
            for d in range(2):
                wgu = jnp.zeros((GATE_PAD, GLA_HK), F32)
                wgu = wgu.at[d * GLA_RANK:(d + 1) * GLA_RANK].set(gla_gate_up[i, d]).astype(BF16)
                bg = gla_gate_bias[i, d].reshape(1, GLA_HK).astype(F32)
                o = None
                for row0, nb, S in segs:
                    o = gla_direction(z, lr, wgu, bg, o, row0, nb, S, rev=(d == 1))
                outs.append(o)
            mixed = gla_post(outs[0], outs[1], z, gla_norm_g[i])
            x = matmul_postnorm_residual(mixed, w_out_odd[i].astype(BF16), ng[1], x)

        q = prenorm_matmul(x, ng[2], w_xq[l].astype(BF16), BF16)
        kv = prenorm_matmul(mem, ng[4], w_xkv[l].astype(BF16), BF16)
        att = None
        for (row0, nb, S), b0 in zip(segs, mem_b0):
            att = cross_attention(q, kv, att, row0, nb, S, b0, n_mem)
        x = matmul_postnorm_residual(att, w_xo[l].astype(BF16), ng[3], x)

        u = prenorm_matmul(x, ng[5], w_up[l].astype(BF16), BF16)
        act = None
        for row0, nb, S in segs:
            act = conv_gate(u, conv_w[l].astype(F32), conv_b[l].astype(F32), act, row0, nb, S)
        x = matmul_postnorm_residual(act, w_down[l].astype(BF16), ng[6], x)

    y_prompt = x[:bp * sp].reshape(bp, sp, D)
    y_sample = x[bp * sp:].reshape(bs, ss, D)
    return (y_prompt, y_sample)
```

```python
import functools
import math

import jax
import jax.numpy as jnp
from jax import lax
from jax.experimental import pallas as pl
from jax.experimental.pallas import tpu as pltpu

F32 = jnp.float32
BF16 = jnp.bfloat16

EPS = 1e-6
DEPTH = 4
LOG2E = math.log2(math.e)
FNET_GROUPS = 4
FNET_GROUP_DIM = 256
FNET_WIDTH = FNET_GROUPS * FNET_GROUP_DIM
DIFF_HEADS = 8
DIFF_HEAD_DIM = 64
DIFF_V_DIM = 2 * DIFF_HEAD_DIM
DIFF_QK = DIFF_HEADS * 2 * DIFF_HEAD_DIM
DIFF_WIDTH = DIFF_HEADS * DIFF_V_DIM
N_BUCKETS = 32
MAX_DISTANCE = 128
DIFF_TQ = 512
DIFF_TK = 1024
GLA_HEADS = 4
GLA_DK = 256
GLA_DV = 512
GLA_RANK = 16
GLA_TAU = 16.0
GLA_HK = GLA_HEADS * GLA_DK
GLA_HV = GLA_HEADS * GLA_DV
GLA_CHUNK = 64
GLA_SUB = 16
GLA_ROWS = 256
GATE_PAD = 128
X_HEADS = 4
X_HEAD_DIM = 512
VMEM_LIMIT_BYTES = 52 * 1024 * 1024
BF16_SUBLANES = 16


def _cparams(*sem):
    return pltpu.CompilerParams(dimension_semantics=sem, vmem_limit_bytes=VMEM_LIMIT_BYTES)


def _pick(n, pref, mult=8):
    if n <= pref:
        return n
    for t in range(pref, 0, -1):
        if n % t == 0 and t % mult == 0:
            return t
    return n


def _rms(x, g):
    return x * lax.rsqrt(jnp.mean(x * x, axis=-1, keepdims=True) + EPS) * g


def _call_into(body, out, args, in_specs, **kw):
    if out is None:
        return pl.pallas_call(body, in_specs=in_specs, **kw)(*args)
    n = len(args)

    def aliased(*refs):
        body(*refs[:n], *refs[n + 1:])

    return pl.pallas_call(aliased, in_specs=list(in_specs) + [pl.BlockSpec(memory_space=pl.ANY)],
                          input_output_aliases={n: 0}, **kw)(*args, out)


def _prenorm_mm_kernel(x_ref, g_ref, w_ref, o_ref, xn_ref):
    @pl.when(pl.program_id(1) == 0)
    def _():
        xn_ref[...] = _rms(x_ref[...], g_ref[...]).astype(BF16)

    o_ref[...] = jnp.dot(xn_ref[...], w_ref[...], preferred_element_type=F32).astype(o_ref.dtype)


def prenorm_matmul(x, g, w, out_dtype, tm=1024, tn=1024):
    M, K = x.shape
    N = w.shape[1]
    tm = _pick(M, tm)
    tn = _pick(N, tn, 128)
    return pl.pallas_call(
        _prenorm_mm_kernel,
        grid=(M // tm, N // tn),
        in_specs=[
            pl.BlockSpec((tm, K), lambda i, j: (i, 0)),
            pl.BlockSpec((1, K), lambda i, j: (0, 0)),
            pl.BlockSpec((K, tn), lambda i, j: (0, j)),
        ],
        out_specs=pl.BlockSpec((tm, tn), lambda i, j: (i, j)),
        out_shape=jax.ShapeDtypeStruct((M, N), out_dtype),
        scratch_shapes=[pltpu.VMEM((tm, K), BF16)],
        compiler_params=_cparams("parallel", "arbitrary"),
        name="prenorm_mm",
    )(x, g.reshape(1, K), w)


def _mm_postnorm_res_kernel(a_ref, w_ref, g_ref, x_ref, o_ref, acc_ref, *, nk):
    k = pl.program_id(1)

    @pl.when(k == 0)
    def _():
        acc_ref[...] = jnp.zeros_like(acc_ref)

    acc_ref[...] += jnp.dot(a_ref[...], w_ref[...], preferred_element_type=F32)

    @pl.when(k == nk - 1)
    def _():
        o_ref[...] = x_ref[...] + _rms(acc_ref[...], g_ref[...])


def _mm_postnorm_res_fullk_kernel(a_ref, w_ref, g_ref, x_ref, o_ref):
    y = jnp.dot(a_ref[...], w_ref[...], preferred_element_type=F32)
    o_ref[...] = x_ref[...] + _rms(y, g_ref[...])


def matmul_postnorm_residual(a, w, g, x, tm=512, tk=2048):
    M, K = a.shape
    D = w.shape[1]
    tm = _pick(M, tm)
    tk = _pick(K, tk, 128)
    nk = K // tk
    if nk == 1:
        return pl.pallas_call(
            _mm_postnorm_res_fullk_kernel,
            grid=(M // tm,),
            in_specs=[
                pl.BlockSpec((tm, K), lambda i: (i, 0)),
                pl.BlockSpec((K, D), lambda i: (0, 0)),
                pl.BlockSpec((1, D), lambda i: (0, 0)),
                pl.BlockSpec((tm, D), lambda i: (i, 0)),
            ],
            out_specs=pl.BlockSpec((tm, D), lambda i: (i, 0)),
            out_shape=jax.ShapeDtypeStruct((M, D), F32),
            compiler_params=_cparams("parallel"),
            name="mm_postnorm_res_fullk",
        )(a, w, g.reshape(1, D), x)
    return pl.pallas_call(
        functools.partial(_mm_postnorm_res_kernel, nk=nk),
        grid=(M // tm, nk),
        in_specs=[
            pl.BlockSpec((tm, tk), lambda i, k: (i, k)),
            pl.BlockSpec((tk, D), lambda i, k: (k, 0)),
            pl.BlockSpec((1, D), lambda i, k: (0, 0)),
            pl.BlockSpec((tm, D), lambda i, k: (i, 0)),
        ],
        out_specs=pl.BlockSpec((tm, D), lambda i, k: (i, 0)),
        out_shape=jax.ShapeDtypeStruct((M, D), F32),
        scratch_shapes=[pltpu.VMEM((tm, D), F32)],
        compiler_params=_cparams("parallel", "arbitrary"),
        name="mm_postnorm_res",
    )(a, w, g.reshape(1, D), x)


def _dft_tables(n, scale):
    split = 1
    while split * split < n:
        split *= 2
    hi = n // split
    k = jnp.arange(n, dtype=jnp.int32)[None, :]
    a = jnp.arange(hi, dtype=jnp.int32)[:, None] * split
    b = jnp.arange(split, dtype=jnp.int32)[:, None]
    ang_a = ((a * k) % n).astype(F32) * (2.0 * math.pi / n)
    ang_b = ((b * k) % n).astype(F32) * (2.0 * math.pi / n)
    ca, sa = jnp.cos(ang_a)[:, None, :], jnp.sin(ang_a)[:, None, :]
    cb, sb = jnp.cos(ang_b)[None, :, :], jnp.sin(ang_b)[None, :, :]
    c = (ca * cb - sa * sb) * scale
    s = (sa * cb + ca * sb) * scale
    return c.reshape(n, n), s.reshape(n, n)


def _fnet_channel_kernel(u_ref, w_ref, o_ref):
    y = jnp.dot(u_ref[...], w_ref[...], preferred_element_type=F32)
    o_ref[0] = y[:, :FNET_GROUP_DIM].astype(BF16)
    o_ref[1] = y[:, FNET_GROUP_DIM:].astype(BF16)


def fnet_channel_dft(z, w_ch, tm=1024):
    M = z.shape[0]
    tm = _pick(M, tm)
    gd = FNET_GROUP_DIM
    return pl.pallas_call(
        _fnet_channel_kernel,
        grid=(M // tm, FNET_GROUPS),
        in_specs=[
            pl.BlockSpec((tm, gd), lambda i, g: (i, g)),
            pl.BlockSpec((gd, 2 * gd), lambda i, g: (0, 0)),
        ],
        out_specs=pl.BlockSpec((2, tm, gd), lambda i, g: (0, i, g)),
        out_shape=jax.ShapeDtypeStruct((2, M, FNET_WIDTH), BF16),
        compiler_params=_cparams("parallel", "parallel"),
        name="fnet_channel",
    )(z, w_ch)


def _fnet_seq_kernel(f_ref, y_ref, o_ref, acc_ref, *, nk):
    k = pl.program_id(2)

    @pl.when(k == 0)
    def _():
        acc_ref[...] = jnp.zeros_like(acc_ref)

    acc_ref[...] += jnp.dot(f_ref[...], y_ref[...], preferred_element_type=F32)

    @pl.when(k == nk - 1)
    def _():
        o_ref[...] = acc_ref[...].astype(o_ref.dtype)


def fnet_seq_dft(fcat, y, out, out_shape, row0, nb, S, tm=1024, tk=512):
    tm = _pick(S, tm)
    tk = _pick(S, tk, 128)
    nks = S // tk
    nk = 2 * nks
    C = y.shape[2]
    rb_k = row0 // tk
    rb_m = row0 // tm
    return _call_into(
        functools.partial(_fnet_seq_kernel, nk=nk), out, (fcat, y),
        in_specs=[
            pl.BlockSpec((tm, tk), lambda b, i, k: (i, k)),
            pl.BlockSpec((None, tk, C), lambda b, i, k: (k // nks, rb_k + b * nks + k % nks, 0)),
        ],
        grid=(nb, S // tm, nk),
        out_specs=pl.BlockSpec((tm, C), lambda b, i, k: (rb_m + b * (S // tm) + i, 0)),
        out_shape=out_shape,
        scratch_shapes=[pltpu.VMEM((tm, C), F32)],
        compiler_params=_cparams("parallel", "parallel", "arbitrary"),
        name="fnet_seq",
    )


def _t5_bucket(rel):
    n = N_BUCKETS // 2
    max_exact = n // 2
    base = jnp.where(rel > 0, n, 0)
    a = jnp.abs(rel)
    af = jnp.maximum(a, 1).astype(F32)
    large = max_exact + (jnp.log(af / max_exact) / math.log(MAX_DISTANCE / max_exact)
                         * (n - max_exact)).astype(jnp.int32)
    large = jnp.minimum(large, n - 1)
    return base + jnp.where(a < max_exact, a, large)


def _bias_tile_kernel(rb_ref, bucket_ref, o_ref):
    h = pl.program_id(1)
    bucket = bucket_ref[...]
    acc = jnp.zeros(bucket.shape, F32)
    for n in range(N_BUCKETS):
        acc = jnp.where(bucket == n, rb_ref[n, h], acc)
    o_ref[...] = acc * LOG2E


def bias_tiles(rel_bias, tq, tk):
    r = tk // tq
    assert tq >= MAX_DISTANCE and tk % tq == 0
    n = 2 * r + 2
    d = jnp.arange(-r - 1, r + 1, dtype=jnp.int32)[:, None, None] * tq
    qq = jnp.arange(tq, dtype=jnp.int32)[None, :, None]
    kk = jnp.arange(tk, dtype=jnp.int32)[None, None, :]
    bucket = _t5_bucket(d + kk - qq)
    return pl.pallas_call(
        _bias_tile_kernel,
        grid=(n, DIFF_HEADS),
        in_specs=[
            pl.BlockSpec(memory_space=pltpu.SMEM),
            pl.BlockSpec((None, tq, tk), lambda t, h: (t, 0, 0)),
        ],
        out_specs=pl.BlockSpec((None, None, tq, tk), lambda t, h: (t, h, 0, 0)),
        out_shape=jax.ShapeDtypeStruct((n, DIFF_HEADS, tq, tk), F32),
        compiler_params=_cparams("parallel", "parallel"),
        name="bias_tiles",
    )(rel_bias.astype(F32), bucket)


def _diff_attn_kernel(q_ref, k_ref, v_ref, bias_ref, lam_ref, hg_ref, o_ref, m_ref, acc_ref,
                      *, nkc, tq, tk, lam_init):
    i = pl.program_id(2)
    r = tk // tq
    vd = DIFF_V_DIM
    m_ref[...] = jnp.full_like(m_ref, -jnp.inf)
    acc_ref[...] = jnp.zeros_like(acc_ref)
    ones = jnp.ones((tk, vd), BF16)

    q = q_ref[...]
    lane = lax.broadcasted_iota(jnp.int32, q.shape, 1)
    zero = jnp.zeros_like(q)
    halves = (jnp.where(lane < DIFF_HEAD_DIM, q, zero), jnp.where(lane >= DIFF_HEAD_DIM, q, zero))

    def chunk(j, carry):
        rows = pl.ds(pl.multiple_of(j * tk, tk), tk)
        k = k_ref[rows, :]
        vx = jnp.concatenate([v_ref[rows, :], ones], axis=1)
        bias = bias_ref[jnp.clip(j * r - i, -r - 1, r) + r + 1]
        for c, qc in enumerate(halves):
            s = lax.dot_general(qc, k, (((1,), (1,)), ((), ())), preferred_element_type=F32) + bias
            m_prev = m_ref[c]
            m_new = jnp.maximum(m_prev, jnp.max(s, axis=-1, keepdims=True))
            alpha = jnp.exp2(m_prev - m_new)
            p = jnp.exp2(s - m_new).astype(BF16)
            acc_ref[c] = alpha * acc_ref[c] + jnp.dot(p, vx, preferred_element_type=F32)
            m_ref[c] = m_new
        return carry

    lax.fori_loop(0, nkc, chunk, 0)

    lp = lam_ref[...]
    lam = (jnp.exp(jnp.sum(lp[0:1] * lp[1:2], axis=-1, keepdims=True))
           - jnp.exp(jnp.sum(lp[2:3] * lp[3:4], axis=-1, keepdims=True)) + lam_init)
    a0 = acc_ref[0]
    a1 = acc_ref[1]
    o = a0[:, :vd] / a0[:, vd:] - lam * (a1[:, :vd] / a1[:, vd:])
    o_ref[...] = (_rms(o, hg_ref[...]) * (1.0 - lam_init)).astype(o_ref.dtype)


def diff_attention(z, bias, lam_p, head_g, out, row0, nb, S, lam_init):
    tq, tk = bias.shape[2], bias.shape[3]
    assert S % tk == 0 and row0 % S == 0
    nq = S // tq
    rb = row0 // tq
    sb = row0 // S
    vd = DIFF_V_DIM
    q0 = FNET_WIDTH // vd
    k0 = (FNET_WIDTH + DIFF_QK) // vd
    v0 = (FNET_WIDTH + 2 * DIFF_QK) // vd
    o0 = FNET_WIDTH // vd
    return _call_into(
        functools.partial(_diff_attn_kernel, nkc=S // tk, tq=tq, tk=tk, lam_init=lam_init), out,
        (z, z, z, bias, lam_p.astype(F32), head_g.reshape(1, vd).astype(F32)),
        in_specs=[
            pl.BlockSpec((tq, vd), lambda b, h, i: (rb + b * nq + i, q0 + h)),
            pl.BlockSpec((S, vd), lambda b, h, i: (sb + b, k0 + h)),
            pl.BlockSpec((S, vd), lambda b, h, i: (sb + b, v0 + h)),
            pl.BlockSpec((bias.shape[0], None, tq, tk), lambda b, h, i: (0, h, 0, 0)),
            pl.BlockSpec((4, DIFF_HEAD_DIM), lambda b, h, i: (0, 0)),
            pl.BlockSpec((1, vd), lambda b, h, i: (0, 0)),
        ],
        grid=(nb, DIFF_HEADS, nq),
        out_specs=pl.BlockSpec((tq, vd), lambda b, h, i: (rb + b * nq + i, o0 + h)),
        out_shape=jax.ShapeDtypeStruct(out.shape, out.dtype),
        scratch_shapes=[
            pltpu.VMEM((2, tq, 1), F32),
            pltpu.VMEM((2, tq, 2 * vd), F32),
        ],
        compiler_params=_cparams("parallel", "parallel", "arbitrary"),
        name="diff_attn",
    )


def _log_sigmoid(x):
    return jnp.minimum(x, 0.0) - jnp.log(1.0 + jnp.exp(-jnp.abs(x)))


def _gla_chunk(q, k, v, pre, st, rev):
    C = GLA_CHUNK
    nt = (((1,), (1,)), ((), ()))
    g = _log_sigmoid(pre) * (1.0 / GLA_TAU)
    row = lax.broadcasted_iota(jnp.int32, (C, C), 0)
    col = lax.broadcasted_iota(jnp.int32, (C, C), 1)
    tri = jnp.where((row <= col) if rev else (row >= col), 1.0, 0.0).astype(BF16)
    g_hi = g.astype(BF16)
    g_lo = (g - g_hi.astype(F32)).astype(BF16)
    b = (jnp.dot(tri, g_hi, preferred_element_type=F32)
         + jnp.dot(tri, g_lo, preferred_element_type=F32))
    edge = 0 if rev else C - 1
    b_edge = b[edge:edge + 1, :]

    qe = (q * jnp.exp(b)).astype(BF16)
    inter = lax.dot_general(qe, st.astype(BF16), nt, preferred_element_type=F32)

    sub = GLA_SUB
    lane = lax.broadcasted_iota(jnp.int32, (sub, C), 1)
    trow = lax.broadcasted_iota(jnp.int32, (sub, 1), 0)
    pieces = []
    for j in range(C // sub):
        lo = j * sub
        hi = lo + sub
        qj = q[lo:hi]
        bj = b[lo:hi]
        aj = jnp.zeros((sub, C), F32)
        for s in range(sub):
            ks = k[lo + s:lo + s + 1]
            bs = b[lo + s:lo + s + 1]
            seen = (trow <= s) if rev else (trow >= s)
            e = jnp.exp(jnp.where(seen, bj - bs, -jnp.inf))
            colv = jnp.sum(qj * ks * e, axis=-1, keepdims=True)
            aj = aj + jnp.where(lane == lo + s, colv, 0.0)
        if (hi < C) if rev else (j > 0):
            mj = b[hi:hi + 1] if rev else b[lo - 1:lo]
            qt = (qj * jnp.exp(bj - mj)).astype(BF16)
            kt = (k * jnp.exp(jnp.minimum(mj - b, 0.0))).astype(BF16)
            off = lax.dot_general(qt, kt, nt, preferred_element_type=F32)
            aj = aj + jnp.where((lane >= hi) if rev else (lane < lo), off, 0.0)
        pieces.append(aj)
    attn = jnp.concatenate(pieces, axis=0)
    intra = jnp.dot(attn.astype(BF16), v, preferred_element_type=F32)

    kd = (k * jnp.exp(b_edge - b)).astype(BF16)
    vt = v.astype(F32).T.astype(BF16)
    st_new = st * jnp.exp(b_edge) + jnp.dot(vt, kd, preferred_element_type=F32)
    return inter + intra, st_new


def _gla_bidir_kernel(qf_ref, kf_ref, vf_ref, lrf_ref, qr_ref, kr_ref, vr_ref, lrr_ref,
                      wguf_ref, wgur_ref, bgf_ref, bgr_ref, of_ref, or_ref, stf_ref, str_ref,
                      *, rows, resets_f, resets_r):
    t = pl.program_id(1)

    def any_of(blocks):
        return functools.reduce(jnp.logical_or, [t == b for b in blocks])

    @pl.when(any_of(resets_f))
    def _():
        stf_ref[...] = jnp.zeros_like(stf_ref)

    @pl.when(any_of(resets_r))
    def _():
        str_ref[...] = jnp.zeros_like(str_ref)

    nc = rows // GLA_CHUNK
    sides = ((qf_ref, kf_ref, vf_ref, lrf_ref, wguf_ref[...], bgf_ref[...], of_ref, stf_ref, False),
             (qr_ref, kr_ref, vr_ref, lrr_ref, wgur_ref[...], bgr_ref[...], or_ref, str_ref, True))

    def step(c, carry):
        for q_ref, k_ref, v_ref, lr_ref, wgu, bg, o_ref, st_ref, rev in sides:
            cc = nc - 1 - c if rev else c
            sl = pl.ds(pl.multiple_of(cc * GLA_CHUNK, GLA_CHUNK), GLA_CHUNK)
            q = q_ref[sl, :].astype(F32) * (GLA_DK ** -0.5)
            k = k_ref[sl, :].astype(F32)
            v = v_ref[sl, :]
            pre = jnp.dot(lr_ref[sl, :].astype(BF16), wgu, preferred_element_type=F32) + bg
            out, st_new = _gla_chunk(q, k, v, pre, st_ref[...], rev)
            o_ref[sl, :] = out
            st_ref[...] = st_new
        return carry

    lax.fori_loop(0, nc, step, 0)


def gla_bidirectional(z, lr, wgu_f, wgu_r, bg_f, bg_r, segs):
    M = z.shape[0]
    rows = GLA_ROWS
    assert all(row0 % rows == 0 and S % rows == 0 for row0, _, S in segs)
    T = M // rows
    starts = [(row0 + b * S) // rows for row0, nb, S in segs for b in range(nb)]
    lasts = [(row0 + (b + 1) * S) // rows - 1 for row0, nb, S in segs for b in range(nb)]
    resets_f = tuple(starts)
    resets_r = tuple(T - 1 - b for b in lasts)
    kb = GLA_HK // GLA_DK
    vb = 2 * GLA_HK // GLA_DV

    def side(rblk):
        return [
            pl.BlockSpec((rows, GLA_DK), lambda h, t: (rblk(t), h)),
            pl.BlockSpec((rows, GLA_DK), lambda h, t: (rblk(t), kb + h)),
            pl.BlockSpec((rows, GLA_DV), lambda h, t: (rblk(t), vb + h)),
            pl.BlockSpec((rows, GATE_PAD), lambda h, t: (rblk(t), 0)),
        ]

    fwd = lambda t: t
    rev = lambda t: T - 1 - t
    wspec = pl.BlockSpec((GATE_PAD, GLA_DK), lambda h, t: (0, h))
    bspec = pl.BlockSpec((1, GLA_DK), lambda h, t: (0, h))
    return pl.pallas_call(
        functools.partial(_gla_bidir_kernel, rows=rows, resets_f=resets_f, resets_r=resets_r),
        grid=(GLA_HEADS, T),
        in_specs=side(fwd) + side(rev) + [wspec, wspec, bspec, bspec],
        out_specs=[pl.BlockSpec((rows, GLA_DV), lambda h, t: (fwd(t), h)),
                   pl.BlockSpec((rows, GLA_DV), lambda h, t: (rev(t), h))],
        out_shape=[jax.ShapeDtypeStruct((M, GLA_HV), F32), jax.ShapeDtypeStruct((M, GLA_HV), F32)],
        scratch_shapes=[pltpu.VMEM((GLA_DV, GLA_DK), F32), pltpu.VMEM((GLA_DV, GLA_DK), F32)],
        compiler_params=_cparams("parallel", "arbitrary"),
        name="gla_bidir",
    )(z, z, z, lr, z, z, z, lr, wgu_f, wgu_r, bg_f, bg_r)


def _gla_post_kernel(of_ref, ob_ref, r_ref, g_ref, o_ref):
    o = _rms(of_ref[...] + ob_ref[...], g_ref[...])
    r = r_ref[...].astype(F32)
    o_ref[...] = (o * (r / (1.0 + jnp.exp(-r)))).astype(o_ref.dtype)


def gla_post(o_f, o_b, z, head_g, tm=512):
    M = o_f.shape[0]
    tm = _pick(M, tm)
    r0 = (2 * GLA_HK + GLA_HV) // GLA_DV
    return pl.pallas_call(
        _gla_post_kernel,
        grid=(M // tm, GLA_HEADS),
        in_specs=[
            pl.BlockSpec((tm, GLA_DV), lambda i, h: (i, h)),
            pl.BlockSpec((tm, GLA_DV), lambda i, h: (i, h)),
            pl.BlockSpec((tm, GLA_DV), lambda i, h: (i, r0 + h)),
            pl.BlockSpec((1, GLA_DV), lambda i, h: (0, 0)),
        ],
        out_specs=pl.BlockSpec((tm, GLA_DV), lambda i, h: (i, h)),
        out_shape=jax.ShapeDtypeStruct((M, GLA_HV), BF16),
        compiler_params=_cparams("parallel", "parallel"),
        name="gla_post",
    )(o_f, o_b, z, head_g.reshape(1, GLA_DV).astype(F32))


def _cross_attn_kernel(q_ref, k_ref, v_ref, o_ref):
    s = lax.dot_general(q_ref[...], k_ref[...], (((1,), (1,)), ((), ())),
                        preferred_element_type=F32) * (X_HEAD_DIM ** -0.5)
    m = jnp.max(s, axis=-1, keepdims=True)
    p = jnp.exp(s - m)
    p = p / jnp.sum(p, axis=-1, keepdims=True)
    o_ref[...] = jnp.dot(p.astype(BF16), v_ref[...], preferred_element_type=F32).astype(o_ref.dtype)


def cross_attention(q, kv, out, row0, nb, S, mem_b0, n_mem, tq=512):
    tq = _pick(S, tq)
    nq = S // tq
    rb = row0 // tq
    hd = X_HEAD_DIM
    return _call_into(
        _cross_attn_kernel, out, (q, kv, kv),
        in_specs=[
            pl.BlockSpec((tq, hd), lambda b, i, h: (rb + b * nq + i, h)),
            pl.BlockSpec((n_mem, hd), lambda b, i, h: (mem_b0 + b, h)),
            pl.BlockSpec((n_mem, hd), lambda b, i, h: (mem_b0 + b, X_HEADS + h)),
        ],
        grid=(nb, nq, X_HEADS),
        out_specs=pl.BlockSpec((tq, hd), lambda b, i, h: (rb + b * nq + i, h)),
        out_shape=jax.ShapeDtypeStruct(q.shape, q.dtype),
        compiler_params=_cparams("parallel", "parallel", "parallel"),
        name="cross_attn",
    )


def _gelu_tanh(x):
    return 0.5 * x * (1.0 + jnp.tanh(math.sqrt(2.0 / math.pi) * (x + 0.044715 * x * x * x)))


def _up_conv_gate_kernel(x_ref, xp_ref, xn_ref, g_ref, wa_ref, wg_ref, cwa_ref, cwg_ref, cba_ref, cbg_ref,
                         o_ref, h_ref, *, tm, firsts, lasts):
    i = pl.program_id(0)
    H = BF16_SUBLANES

    def any_of(tiles):
        return functools.reduce(jnp.logical_or, [i == t for t in tiles])

    @pl.when(pl.program_id(1) == 0)
    def _():
        g = g_ref[...]
        h_ref[H:H + tm, :] = _rms(x_ref[...], g).astype(BF16)
        keep_prev = jnp.where(any_of(firsts), 0.0, 1.0)
        keep_next = jnp.where(any_of(lasts), 0.0, 1.0)
        h_ref[0:H, :] = (_rms(xp_ref[...], g) * keep_prev).astype(BF16)
        h_ref[H + tm:2 * H + tm, :] = (_rms(xn_ref[...], g) * keep_next).astype(BF16)

    h = h_ref[...]

    def conv(w_ref, cw_ref, cb_ref):
        u = jnp.dot(h, w_ref[...], preferred_element_type=F32)
        cw = cw_ref[...]
        return (cw[0:1] * u[H - 1:H - 1 + tm] + cw[1:2] * u[H:H + tm] + cw[2:3] * u[H + 1:H + 1 + tm]
                + cb_ref[...])

    a = conv(wa_ref, cwa_ref, cba_ref)
    gate = conv(wg_ref, cwg_ref, cbg_ref)
    o_ref[...] = (_gelu_tanh(gate) * a).astype(o_ref.dtype)


def up_conv_gate(x, g, w_up, conv_w, conv_b, segs, tm=1024, tn=512):
    M, K = x.shape
    F = w_up.shape[1] // 2
    H = BF16_SUBLANES
    tm = _pick(math.gcd(*[S for _, _, S in segs]), tm, H)
    tn = _pick(F, tn, 128)
    assert all(row0 % tm == 0 for row0, _, _ in segs)
    nj = F // tn
    hb = tm // H
    firsts = tuple((row0 + b * S) // tm for row0, nb, S in segs for b in range(nb))
    lasts = tuple((row0 + (b + 1) * S) // tm - 1 for row0, nb, S in segs for b in range(nb))
    return pl.pallas_call(
        functools.partial(_up_conv_gate_kernel, tm=tm, firsts=firsts, lasts=lasts),
        grid=(M // tm, nj),
        in_specs=[
            pl.BlockSpec((tm, K), lambda i, j: (i, 0)),
            pl.BlockSpec((H, K), lambda i, j: (jnp.maximum(i * hb - 1, 0), 0)),
            pl.BlockSpec((H, K), lambda i, j: (jnp.minimum((i + 1) * hb, M // H - 1), 0)),
            pl.BlockSpec((1, K), lambda i, j: (0, 0)),
            pl.BlockSpec((K, tn), lambda i, j: (0, j)),
            pl.BlockSpec((K, tn), lambda i, j: (0, nj + j)),
            pl.BlockSpec((3, tn), lambda i, j: (0, j)),
            pl.BlockSpec((3, tn), lambda i, j: (0, nj + j)),
            pl.BlockSpec((1, tn), lambda i, j: (0, j)),
            pl.BlockSpec((1, tn), lambda i, j: (0, nj + j)),
        ],
        out_specs=pl.BlockSpec((tm, tn), lambda i, j: (i, j)),
        out_shape=jax.ShapeDtypeStruct((M, F), BF16),
        scratch_shapes=[pltpu.VMEM((tm + 2 * H, K), BF16)],
        compiler_params=_cparams("parallel", "arbitrary"),
        name="up_conv_gate",
    )(x, x, x, g.reshape(1, K), w_up, w_up, conv_w, conv_w, conv_b.reshape(1, 2 * F), conv_b.reshape(1, 2 * F))


def kernel(x_prompt, x_sample, mem_prompt, mem_sample, norm_g, rel_bias, w_in_even, w_out_even,
           diff_lambda, diff_norm_g, w_in_odd, gla_gate_down, gla_gate_up, gla_gate_bias, gla_norm_g,
           w_out_odd, w_xq, w_xkv, w_xo, w_up, conv_w, conv_b, w_down):
    bp, sp, D = x_prompt.shape
    bs, ss, _ = x_sample.shape
    n_mem = mem_prompt.shape[1]
    segs = ((0, bp, sp), (bp * sp, bs, ss))
    M = bp * sp + bs * ss
    x = jnp.concatenate([x_prompt.reshape(bp * sp, D), x_sample.reshape(bs * ss, D)], axis=0)
    mem = jnp.concatenate([mem_prompt.reshape(bp * n_mem, D), mem_sample.reshape(bs * n_mem, D)], axis=0)
    mem_b0 = (0, bp)
    norm_g = norm_g.astype(F32)

    tk = min(DIFF_TK, sp, ss)
    bias = bias_tiles(rel_bias, min(DIFF_TQ, tk), tk)
    gd = FNET_GROUP_DIM
    c_ch, s_ch = _dft_tables(gd, gd ** -0.5)
    w_ch = jnp.concatenate([c_ch, -s_ch], axis=1).astype(BF16)
    fcat = {}
    for _, _, S in segs:
        if S not in fcat:
            c_s, s_s = _dft_tables(S, S ** -0.5)
            fcat[S] = jnp.concatenate([c_s, s_s], axis=1).astype(BF16)
    q_cols = (jnp.arange(w_in_even.shape[2]) >= FNET_WIDTH) & (jnp.arange(w_in_even.shape[2]) < FNET_WIDTH + DIFF_QK)
    col_scale = jnp.where(q_cols, DIFF_HEAD_DIM ** -0.5 * LOG2E, 1.0).astype(F32)

    for l in range(DEPTH):
        ng = norm_g[l]
        i = l // 2
        if l % 2 == 0:
            lam_init = 0.8 - 0.6 * math.exp(-0.3 * l)
            z = prenorm_matmul(x, ng[0], (w_in_even[i] * col_scale).astype(BF16), BF16)
            y = fnet_channel_dft(z, w_ch)
            mixed = None
            mixed_shape = jax.ShapeDtypeStruct((M, FNET_WIDTH + DIFF_WIDTH), BF16)
            for row0, nb, S in segs:
                mixed = fnet_seq_dft(fcat[S], y, mixed, mixed_shape, row0, nb, S)
            for row0, nb, S in segs:
                mixed = diff_attention(z, bias, diff_lambda[i], diff_norm_g[i], mixed, row0, nb, S, lam_init)
            x = matmul_postnorm_residual(mixed, w_out_even[i].astype(BF16), ng[1], x)
        else:
            z = prenorm_matmul(x, ng[0], w_in_odd[i].astype(BF16), BF16)
            w_gd = jnp.concatenate([gla_gate_down[i, 0], gla_gate_down[i, 1]], axis=1)
            w_gd = jnp.pad(w_gd, ((0, 0), (0, GATE_PAD - 2 * GLA_RANK))).astype(BF16)
            lr = prenorm_matmul(x, ng[0], w_gd, F32)
            wgu, bg = [], []
            for d in range(2):
                pad = ((d * GLA_RANK, GATE_PAD - (d + 1) * GLA_RANK), (0, 0))
                wgu.append(jnp.pad(gla_gate_up[i, d], pad).astype(BF16))
                bg.append(gla_gate_bias[i, d].reshape(1, GLA_HK).astype(F32))
            o_f, o_b = gla_bidirectional(z, lr, wgu[0], wgu[1], bg[0], bg[1], segs)
            mixed = gla_post(o_f, o_b, z, gla_norm_g[i])
            x = matmul_postnorm_residual(mixed, w_out_odd[i].astype(BF16), ng[1], x)

        q = prenorm_matmul(x, ng[2], w_xq[l].astype(BF16), BF16)
        kv = prenorm_matmul(mem, ng[4], w_xkv[l].astype(BF16), BF16)
        att = None
        for (row0, nb, S), b0 in zip(segs, mem_b0):
            att = cross_attention(q, kv, att, row0, nb, S, b0, n_mem)
        x = matmul_postnorm_residual(att, w_xo[l].astype(BF16), ng[3], x)

        act = up_conv_gate(x, ng[5], w_up[l].astype(BF16), conv_w[l].astype(F32), conv_b[l].astype(F32), segs)
        x = matmul_postnorm_residual(act, w_down[l].astype(BF16), ng[6], x)

    y_prompt = x[:bp * sp].reshape(bp, sp, D)
    y_sample = x[bp * sp:].reshape(bs, ss, D)
    return (y_prompt, y_sample)
```

```python
import functools
import math

import jax
import jax.numpy as jnp
from jax import lax
from jax.experimental import pallas as pl
from jax.experimental.pallas import tpu as pltpu

F32 = jnp.float32
BF16 = jnp.bfloat16

EPS = 1e-6
DEPTH = 4
LOG2E = math.log2(math.e)
FNET_GROUPS = 4
FNET_GROUP_DIM = 256
FNET_WIDTH = FNET_GROUPS * FNET_GROUP_DIM
DIFF_HEADS = 8
DIFF_HEAD_DIM = 64
DIFF_V_DIM = 2 * DIFF_HEAD_DIM
DIFF_QK = DIFF_HEADS * 2 * DIFF_HEAD_DIM
DIFF_WIDTH = DIFF_HEADS * DIFF_V_DIM
N_BUCKETS = 32
MAX_DISTANCE = 128
DIFF_TQ = 512
DIFF_TK = 1024
GLA_HEADS = 4
GLA_DK = 256
GLA_DV = 512
GLA_RANK = 16
GLA_TAU = 16.0
GLA_HK = GLA_HEADS * GLA_DK
GLA_HV = GLA_HEADS * GLA_DV
GLA_CHUNK = 64
GLA_SUB = 16
GLA_ROWS = 256
GATE_PAD = 128
X_HEADS = 4
X_HEAD_DIM = 512
VMEM_LIMIT_BYTES = 52 * 1024 * 1024
BF16_SUBLANES = 16


def _cparams(*sem):
    return pltpu.CompilerParams(dimension_semantics=sem, vmem_limit_bytes=VMEM_LIMIT_BYTES)


def _pick(n, pref, mult=8):
    if n <= pref:
        return n
    for t in range(pref, 0, -1):
        if n % t == 0 and t % mult == 0:
            return t
    return n


def _rms(x, g):
    return x * lax.rsqrt(jnp.mean(x * x, axis=-1, keepdims=True) + EPS) * g


def _call_into(body, out, args, in_specs, **kw):
    if out is None:
        return pl.pallas_call(body, in_specs=in_specs, **kw)(*args)
    n = len(args)

    def aliased(*refs):
        body(*refs[:n], *refs[n + 1:])

    return pl.pallas_call(aliased, in_specs=list(in_specs) + [pl.BlockSpec(memory_space=pl.ANY)],
                          input_output_aliases={n: 0}, **kw)(*args, out)


def _prenorm_mm_kernel(x_ref, g_ref, w_ref, o_ref, xn_ref):
    @pl.when(pl.program_id(1) == 0)
    def _():
        xn_ref[...] = _rms(x_ref[...], g_ref[...]).astype(BF16)

    o_ref[...] = jnp.dot(xn_ref[...], w_ref[...], preferred_element_type=F32).astype(o_ref.dtype)


def prenorm_matmul(x, g, w, out_dtype, tm=1024, tn=1024):
    M, K = x.shape
    N = w.shape[1]
    tm = _pick(M, tm)
    tn = _pick(N, tn, 128)
    return pl.pallas_call(
        _prenorm_mm_kernel,
        grid=(M // tm, N // tn),
        in_specs=[
            pl.BlockSpec((tm, K), lambda i, j: (i, 0)),
            pl.BlockSpec((1, K), lambda i, j: (0, 0)),
            pl.BlockSpec((K, tn), lambda i, j: (0, j)),
        ],
        out_specs=pl.BlockSpec((tm, tn), lambda i, j: (i, j)),
        out_shape=jax.ShapeDtypeStruct((M, N), out_dtype),
        scratch_shapes=[pltpu.VMEM((tm, K), BF16)],
        compiler_params=_cparams("parallel", "arbitrary"),
        name="prenorm_mm",
    )(x, g.reshape(1, K), w)


def _mm_postnorm_res_kernel(a_ref, w_ref, g_ref, x_ref, o_ref, acc_ref, *, nk):
    k = pl.program_id(1)

    @pl.when(k == 0)
    def _():
        acc_ref[...] = jnp.zeros_like(acc_ref)

    acc_ref[...] += jnp.dot(a_ref[...], w_ref[...], preferred_element_type=F32)

    @pl.when(k == nk - 1)
    def _():
        o_ref[...] = x_ref[...] + _rms(acc_ref[...], g_ref[...])


def _mm_postnorm_res_fullk_kernel(a_ref, w_ref, g_ref, x_ref, o_ref):
    y = jnp.dot(a_ref[...], w_ref[...], preferred_element_type=F32)
    o_ref[...] = x_ref[...] + _rms(y, g_ref[...])


def matmul_postnorm_residual(a, w, g, x, tm=512, tk=2048):
    M, K = a.shape
    D = w.shape[1]
    tm = _pick(M, tm)
    tk = _pick(K, tk, 128)
    nk = K // tk
    if nk == 1:
        return pl.pallas_call(
            _mm_postnorm_res_fullk_kernel,
            grid=(M // tm,),
            in_specs=[
                pl.BlockSpec((tm, K), lambda i: (i, 0)),
                pl.BlockSpec((K, D), lambda i: (0, 0)),
                pl.BlockSpec((1, D), lambda i: (0, 0)),
                pl.BlockSpec((tm, D), lambda i: (i, 0)),
            ],
            out_specs=pl.BlockSpec((tm, D), lambda i: (i, 0)),
            out_shape=jax.ShapeDtypeStruct((M, D), F32),
            compiler_params=_cparams("parallel"),
            name="mm_postnorm_res_fullk",
        )(a, w, g.reshape(1, D), x)
    return pl.pallas_call(
        functools.partial(_mm_postnorm_res_kernel, nk=nk),
        grid=(M // tm, nk),
        in_specs=[
            pl.BlockSpec((tm, tk), lambda i, k: (i, k)),
            pl.BlockSpec((tk, D), lambda i, k: (k, 0)),
            pl.BlockSpec((1, D), lambda i, k: (0, 0)),
            pl.BlockSpec((tm, D), lambda i, k: (i, 0)),
        ],
        out_specs=pl.BlockSpec((tm, D), lambda i, k: (i, 0)),
        out_shape=jax.ShapeDtypeStruct((M, D), F32),
        scratch_shapes=[pltpu.VMEM((tm, D), F32)],
        compiler_params=_cparams("parallel", "arbitrary"),
        name="mm_postnorm_res",
    )(a, w, g.reshape(1, D), x)


def _dft_tables(n, scale):
    split = 1
    while split * split < n:
        split *= 2
    hi = n // split
    k = jnp.arange(n, dtype=jnp.int32)[None, :]
    a = jnp.arange(hi, dtype=jnp.int32)[:, None] * split
    b = jnp.arange(split, dtype=jnp.int32)[:, None]
    ang_a = ((a * k) % n).astype(F32) * (2.0 * math.pi / n)
    ang_b = ((b * k) % n).astype(F32) * (2.0 * math.pi / n)
    ca, sa = jnp.cos(ang_a)[:, None, :], jnp.sin(ang_a)[:, None, :]
    cb, sb = jnp.cos(ang_b)[None, :, :], jnp.sin(ang_b)[None, :, :]
    c = (ca * cb - sa * sb) * scale
    s = (sa * cb + ca * sb) * scale
    return c.reshape(n, n), s.reshape(n, n)


def _fnet_channel_kernel(u_ref, w_ref, o_ref):
    y = jnp.dot(u_ref[...], w_ref[...], preferred_element_type=F32)
    o_ref[0] = y[:, :FNET_GROUP_DIM].astype(BF16)
    o_ref[1] = y[:, FNET_GROUP_DIM:].astype(BF16)


def fnet_channel_dft(z, w_ch, tm=1024):
    M = z.shape[0]
    tm = _pick(M, tm)
    gd = FNET_GROUP_DIM
    return pl.pallas_call(
        _fnet_channel_kernel,
        grid=(M // tm, FNET_GROUPS),
        in_specs=[
            pl.BlockSpec((tm, gd), lambda i, g: (i, g)),
            pl.BlockSpec((gd, 2 * gd), lambda i, g: (0, 0)),
        ],
        out_specs=pl.BlockSpec((2, tm, gd), lambda i, g: (0, i, g)),
        out_shape=jax.ShapeDtypeStruct((2, M, FNET_WIDTH), BF16),
        compiler_params=_cparams("parallel", "parallel"),
        name="fnet_channel",
    )(z, w_ch)


def _fnet_seq_kernel(f_ref, y_ref, o_ref, acc_ref, *, nk):
    k = pl.program_id(2)

    @pl.when(k == 0)
    def _():
        acc_ref[...] = jnp.zeros_like(acc_ref)

    acc_ref[...] += jnp.dot(f_ref[...], y_ref[...], preferred_element_type=F32)

    @pl.when(k == nk - 1)
    def _():
        o_ref[...] = acc_ref[...].astype(o_ref.dtype)


def fnet_seq_dft(fcat, y, out, out_shape, row0, nb, S, tm=1024, tk=512):
    tm = _pick(S, tm)
    tk = _pick(S, tk, 128)
    nks = S // tk
    nk = 2 * nks
    C = y.shape[2]
    rb_k = row0 // tk
    rb_m = row0 // tm
    return _call_into(
        functools.partial(_fnet_seq_kernel, nk=nk), out, (fcat, y),
        in_specs=[
            pl.BlockSpec((tm, tk), lambda b, i, k: (i, k)),
            pl.BlockSpec((None, tk, C), lambda b, i, k: (k // nks, rb_k + b * nks + k % nks, 0)),
        ],
        grid=(nb, S // tm, nk),
        out_specs=pl.BlockSpec((tm, C), lambda b, i, k: (rb_m + b * (S // tm) + i, 0)),
        out_shape=out_shape,
        scratch_shapes=[pltpu.VMEM((tm, C), F32)],
        compiler_params=_cparams("parallel", "parallel", "arbitrary"),
        name="fnet_seq",
    )


def _t5_bucket(rel):
    n = N_BUCKETS // 2
    max_exact = n // 2
    base = jnp.where(rel > 0, n, 0)
    a = jnp.abs(rel)
    af = jnp.maximum(a, 1).astype(F32)
    large = max_exact + (jnp.log(af / max_exact) / math.log(MAX_DISTANCE / max_exact)
                         * (n - max_exact)).astype(jnp.int32)
    large = jnp.minimum(large, n - 1)
    return base + jnp.where(a < max_exact, a, large)


def _bias_tile_kernel(rb_ref, bucket_ref, o_ref):
    h = pl.program_id(1)
    bucket = bucket_ref[...]
    acc = jnp.zeros(bucket.shape, F32)
    for n in range(N_BUCKETS):
        acc = jnp.where(bucket == n, rb_ref[n, h], acc)
    o_ref[...] = acc * LOG2E


def bias_tiles(rel_bias, tq, tk):
    r = tk // tq
    assert tq >= MAX_DISTANCE and tk % tq == 0
    n = 2 * r + 2
    d = jnp.arange(-r - 1, r + 1, dtype=jnp.int32)[:, None, None] * tq
    qq = jnp.arange(tq, dtype=jnp.int32)[None, :, None]
    kk = jnp.arange(tk, dtype=jnp.int32)[None, None, :]
    bucket = _t5_bucket(d + kk - qq)
    return pl.pallas_call(
        _bias_tile_kernel,
        grid=(n, DIFF_HEADS),
        in_specs=[
            pl.BlockSpec(memory_space=pltpu.SMEM),
            pl.BlockSpec((None, tq, tk), lambda t, h: (t, 0, 0)),
        ],
        out_specs=pl.BlockSpec((None, None, tq, tk), lambda t, h: (t, h, 0, 0)),
        out_shape=jax.ShapeDtypeStruct((n, DIFF_HEADS, tq, tk), F32),
        compiler_params=_cparams("parallel", "parallel"),
        name="bias_tiles",
    )(rel_bias.astype(F32), bucket)


def _diff_attn_kernel(q_ref, k_ref, v_ref, bias_ref, lam_ref, hg_ref, o_ref, m_ref, acc_ref,
                      *, nkc, tq, tk, lam_init):
    i = pl.program_id(2)
    r = tk // tq
    vd = DIFF_V_DIM
    m_ref[...] = jnp.full_like(m_ref, -jnp.inf)
    acc_ref[...] = jnp.zeros_like(acc_ref)
    ones = jnp.ones((tk, vd), BF16)

    q = q_ref[...]
    lane = lax.broadcasted_iota(jnp.int32, q.shape, 1)
    zero = jnp.zeros_like(q)
    halves = (jnp.where(lane < DIFF_HEAD_DIM, q, zero), jnp.where(lane >= DIFF_HEAD_DIM, q, zero))

    def chunk(j, carry):
        rows = pl.ds(pl.multiple_of(j * tk, tk), tk)
        k = k_ref[rows, :]
        vx = jnp.concatenate([v_ref[rows, :], ones], axis=1)
        bias = bias_ref[jnp.clip(j * r - i, -r - 1, r) + r + 1]
        for c, qc in enumerate(halves):
            s = lax.dot_general(qc, k, (((1,), (1,)), ((), ())), preferred_element_type=F32) + bias
            m_prev = m_ref[c]
            m_new = jnp.maximum(m_prev, jnp.max(s, axis=-1, keepdims=True))
            alpha = jnp.exp2(m_prev - m_new)
            p = jnp.exp2(s - m_new).astype(BF16)
            acc_ref[c] = alpha * acc_ref[c] + jnp.dot(p, vx, preferred_element_type=F32)
            m_ref[c] = m_new
        return carry

    lax.fori_loop(0, nkc, chunk, 0)

    lp = lam_ref[...]
    lam = (jnp.exp(jnp.sum(lp[0:1] * lp[1:2], axis=-1, keepdims=True))
           - jnp.exp(jnp.sum(lp[2:3] * lp[3:4], axis=-1, keepdims=True)) + lam_init)
    a0 = acc_ref[0]
    a1 = acc_ref[1]
    o = a0[:, :vd] / a0[:, vd:] - lam * (a1[:, :vd] / a1[:, vd:])
    o_ref[...] = (_rms(o, hg_ref[...]) * (1.0 - lam_init)).astype(o_ref.dtype)


def diff_attention(z, bias, lam_p, head_g, out, row0, nb, S, lam_init):
    tq, tk = bias.shape[2], bias.shape[3]
    assert S % tk == 0 and row0 % S == 0
    nq = S // tq
    rb = row0 // tq
    sb = row0 // S
    vd = DIFF_V_DIM
    q0 = FNET_WIDTH // vd
    k0 = (FNET_WIDTH + DIFF_QK) // vd
    v0 = (FNET_WIDTH + 2 * DIFF_QK) // vd
    o0 = FNET_WIDTH // vd
    return _call_into(
        functools.partial(_diff_attn_kernel, nkc=S // tk, tq=tq, tk=tk, lam_init=lam_init), out,
        (z, z, z, bias, lam_p.astype(F32), head_g.reshape(1, vd).astype(F32)),
        in_specs=[
            pl.BlockSpec((tq, vd), lambda b, h, i: (rb + b * nq + i, q0 + h)),
            pl.BlockSpec((S, vd), lambda b, h, i: (sb + b, k0 + h)),
            pl.BlockSpec((S, vd), lambda b, h, i: (sb + b, v0 + h)),
            pl.BlockSpec((bias.shape[0], None, tq, tk), lambda b, h, i: (0, h, 0, 0)),
            pl.BlockSpec((4, DIFF_HEAD_DIM), lambda b, h, i: (0, 0)),
            pl.BlockSpec((1, vd), lambda b, h, i: (0, 0)),
        ],
        grid=(nb, DIFF_HEADS, nq),
        out_specs=pl.BlockSpec((tq, vd), lambda b, h, i: (rb + b * nq + i, o0 + h)),
        out_shape=jax.ShapeDtypeStruct(out.shape, out.dtype),
        scratch_shapes=[
            pltpu.VMEM((2, tq, 1), F32),
            pltpu.VMEM((2, tq, 2 * vd), F32),
        ],
        compiler_params=_cparams("parallel", "parallel", "arbitrary"),
        name="diff_attn",
    )


def _log_sigmoid(x):
    return jnp.minimum(x, 0.0) - jnp.log(1.0 + jnp.exp(-jnp.abs(x)))


def _gla_chunk(q, k, v, pre, st, rev):
    C = GLA_CHUNK
    nt = (((1,), (1,)), ((), ()))
    g = _log_sigmoid(pre) * (1.0 / GLA_TAU)
    row = lax.broadcasted_iota(jnp.int32, (C, C), 0)
    col = lax.broadcasted_iota(jnp.int32, (C, C), 1)
    tri = jnp.where((row <= col) if rev else (row >= col), 1.0, 0.0).astype(BF16)
    g_hi = g.astype(BF16)
    g_lo = (g - g_hi.astype(F32)).astype(BF16)
    b = (jnp.dot(tri, g_hi, preferred_element_type=F32)
         + jnp.dot(tri, g_lo, preferred_element_type=F32))
    edge = 0 if rev else C - 1
    b_edge = b[edge:edge + 1, :]

    qe = (q * jnp.exp(b)).astype(BF16)
    inter = lax.dot_general(qe, st.astype(BF16), nt, preferred_element_type=F32)

    sub = GLA_SUB
    lane = lax.broadcasted_iota(jnp.int32, (sub, C), 1)
    trow = lax.broadcasted_iota(jnp.int32, (sub, 1), 0)
    pieces = []
    for j in range(C // sub):
        lo = j * sub
        hi = lo + sub
        qj = q[lo:hi]
        bj = b[lo:hi]
        aj = jnp.zeros((sub, C), F32)
        for s in range(sub):
            ks = k[lo + s:lo + s + 1]
            bs = b[lo + s:lo + s + 1]
            seen = (trow <= s) if rev else (trow >= s)
            e = jnp.exp(jnp.where(seen, bj - bs, -jnp.inf))
            colv = jnp.sum(qj * ks * e, axis=-1, keepdims=True)
            aj = aj + jnp.where(lane == lo + s, colv, 0.0)
        if (hi < C) if rev else (j > 0):
            mj = b[hi:hi + 1] if rev else b[lo - 1:lo]
            qt = (qj * jnp.exp(bj - mj)).astype(BF16)
            kt = (k * jnp.exp(jnp.minimum(mj - b, 0.0))).astype(BF16)
            off = lax.dot_general(qt, kt, nt, preferred_element_type=F32)
            aj = aj + jnp.where((lane >= hi) if rev else (lane < lo), off, 0.0)
        pieces.append(aj)
    attn = jnp.concatenate(pieces, axis=0)
    intra = jnp.dot(attn.astype(BF16), v, preferred_element_type=F32)

    kd = (k * jnp.exp(b_edge - b)).astype(BF16)
    vt = v.astype(F32).T.astype(BF16)
    st_new = st * jnp.exp(b_edge) + jnp.dot(vt, kd, preferred_element_type=F32)
    return inter + intra, st_new


def _gla_bidir_kernel(qf_ref, kf_ref, vf_ref, lrf_ref, qr_ref, kr_ref, vr_ref, lrr_ref,
                      wguf_ref, wgur_ref, bgf_ref, bgr_ref, of_ref, or_ref, stf_ref, str_ref,
                      *, rows, resets_f, resets_r):
    t = pl.program_id(1)

    def any_of(blocks):
        return functools.reduce(jnp.logical_or, [t == b for b in blocks])

    @pl.when(any_of(resets_f))
    def _():
        stf_ref[...] = jnp.zeros_like(stf_ref)

    @pl.when(any_of(resets_r))
    def _():
        str_ref[...] = jnp.zeros_like(str_ref)

    nc = rows // GLA_CHUNK
    sides = ((qf_ref, kf_ref, vf_ref, lrf_ref, wguf_ref[...], bgf_ref[...], of_ref, stf_ref, False),
             (qr_ref, kr_ref, vr_ref, lrr_ref, wgur_ref[...], bgr_ref[...], or_ref, str_ref, True))

    def step(c, carry):
        for q_ref, k_ref, v_ref, lr_ref, wgu, bg, o_ref, st_ref, rev in sides:
            cc = nc - 1 - c if rev else c
            sl = pl.ds(pl.multiple_of(cc * GLA_CHUNK, GLA_CHUNK), GLA_CHUNK)
            q = q_ref[sl, :].astype(F32) * (GLA_DK ** -0.5)
            k = k_ref[sl, :].astype(F32)
            v = v_ref[sl, :]
            pre = jnp.dot(lr_ref[sl, :].astype(BF16), wgu, preferred_element_type=F32) + bg
            out, st_new = _gla_chunk(q, k, v, pre, st_ref[...], rev)
            o_ref[sl, :] = out
            st_ref[...] = st_new
        return carry

    lax.fori_loop(0, nc, step, 0, unroll=True)


def gla_bidirectional(z, lr, wgu_f, wgu_r, bg_f, bg_r, segs):
    M = z.shape[0]
    rows = GLA_ROWS
    assert all(row0 % rows == 0 and S % rows == 0 for row0, _, S in segs)
    T = M // rows
    starts = [(row0 + b * S) // rows for row0, nb, S in segs for b in range(nb)]
    lasts = [(row0 + (b + 1) * S) // rows - 1 for row0, nb, S in segs for b in range(nb)]
    resets_f = tuple(starts)
    resets_r = tuple(T - 1 - b for b in lasts)
    kb = GLA_HK // GLA_DK
    vb = 2 * GLA_HK // GLA_DV

    def side(rblk):
        return [
            pl.BlockSpec((rows, GLA_DK), lambda h, t: (rblk(t), h)),
            pl.BlockSpec((rows, GLA_DK), lambda h, t: (rblk(t), kb + h)),
            pl.BlockSpec((rows, GLA_DV), lambda h, t: (rblk(t), vb + h)),
            pl.BlockSpec((rows, GATE_PAD), lambda h, t: (rblk(t), 0)),
        ]

    fwd = lambda t: t
    rev = lambda t: T - 1 - t
    wspec = pl.BlockSpec((GATE_PAD, GLA_DK), lambda h, t: (0, h))
    bspec = pl.BlockSpec((1, GLA_DK), lambda h, t: (0, h))
    return pl.pallas_call(
        functools.partial(_gla_bidir_kernel, rows=rows, resets_f=resets_f, resets_r=resets_r),
        grid=(GLA_HEADS, T),
        in_specs=side(fwd) + side(rev) + [wspec, wspec, bspec, bspec],
        out_specs=[pl.BlockSpec((rows, GLA_DV), lambda h, t: (fwd(t), h)),
                   pl.BlockSpec((rows, GLA_DV), lambda h, t: (rev(t), h))],
        out_shape=[jax.ShapeDtypeStruct((M, GLA_HV), F32), jax.ShapeDtypeStruct((M, GLA_HV), F32)],
        scratch_shapes=[pltpu.VMEM((GLA_DV, GLA_DK), F32), pltpu.VMEM((GLA_DV, GLA_DK), F32)],
        compiler_params=_cparams("parallel", "arbitrary"),
        name="gla_bidir",
    )(z, z, z, lr, z, z, z, lr, wgu_f, wgu_r, bg_f, bg_r)


def _gla_post_kernel(of_ref, ob_ref, r_ref, g_ref, o_ref):
    o = _rms(of_ref[...] + ob_ref[...], g_ref[...])
    r = r_ref[...].astype(F32)
    o_ref[...] = (o * (r / (1.0 + jnp.exp(-r)))).astype(o_ref.dtype)


def gla_post(o_f, o_b, z, head_g, tm=512):
    M = o_f.shape[0]
    tm = _pick(M, tm)
    r0 = (2 * GLA_HK + GLA_HV) // GLA_DV
    return pl.pallas_call(
        _gla_post_kernel,
        grid=(M // tm, GLA_HEADS),
        in_specs=[
            pl.BlockSpec((tm, GLA_DV), lambda i, h: (i, h)),
            pl.BlockSpec((tm, GLA_DV), lambda i, h: (i, h)),
            pl.BlockSpec((tm, GLA_DV), lambda i, h: (i, r0 + h)),
            pl.BlockSpec((1, GLA_DV), lambda i, h: (0, 0)),
        ],
        out_specs=pl.BlockSpec((tm, GLA_DV), lambda i, h: (i, h)),
        out_shape=jax.ShapeDtypeStruct((M, GLA_HV), BF16),
        compiler_params=_cparams("parallel", "parallel"),
        name="gla_post",
    )(o_f, o_b, z, head_g.reshape(1, GLA_DV).astype(F32))


def _cross_attn_kernel(q_ref, k_ref, v_ref, o_ref):
    for h in range(X_HEADS):
        cols = slice(h * X_HEAD_DIM, (h + 1) * X_HEAD_DIM)
        s = lax.dot_general(q_ref[:, cols], k_ref[:, cols], (((1,), (1,)), ((), ())),
                            preferred_element_type=F32) * (X_HEAD_DIM ** -0.5)
        m = jnp.max(s, axis=-1, keepdims=True)
        p = jnp.exp(s - m)
        p = p / jnp.sum(p, axis=-1, keepdims=True)
        o_ref[:, cols] = jnp.dot(p.astype(BF16), v_ref[:, cols],
                                 preferred_element_type=F32).astype(o_ref.dtype)


def cross_attention(q, kv, out, row0, nb, S, mem_b0, n_mem, tq=512):
    tq = _pick(S, tq)
    nq = S // tq
    rb = row0 // tq
    width = X_HEADS * X_HEAD_DIM
    return _call_into(
        _cross_attn_kernel, out, (q, kv, kv),
        in_specs=[
            pl.BlockSpec((tq, width), lambda b, i: (rb + b * nq + i, 0)),
            pl.BlockSpec((n_mem, width), lambda b, i: (mem_b0 + b, 0)),
            pl.BlockSpec((n_mem, width), lambda b, i: (mem_b0 + b, 1)),
        ],
        grid=(nb, nq),
        out_specs=pl.BlockSpec((tq, width), lambda b, i: (rb + b * nq + i, 0)),
        out_shape=jax.ShapeDtypeStruct(q.shape, q.dtype),
        compiler_params=_cparams("parallel", "parallel"),
        name="cross_attn",
    )


def _gelu_tanh(x):
    return 0.5 * x * (1.0 + jnp.tanh(math.sqrt(2.0 / math.pi) * (x + 0.044715 * x * x * x)))


def _up_conv_gate_kernel(x_ref, xp_ref, xn_ref, g_ref, wa_ref, wg_ref, cwa_ref, cwg_ref, cba_ref, cbg_ref,
                         o_ref, h_ref, *, tm, firsts, lasts):
    i = pl.program_id(0)
    H = BF16_SUBLANES

    def any_of(tiles):
        return functools.reduce(jnp.logical_or, [i == t for t in tiles])

    @pl.when(pl.program_id(1) == 0)
    def _():
        g = g_ref[...]
        h_ref[H:H + tm, :] = _rms(x_ref[...], g).astype(BF16)
        keep_prev = jnp.where(any_of(firsts), 0.0, 1.0)
        keep_next = jnp.where(any_of(lasts), 0.0, 1.0)
        h_ref[0:H, :] = (_rms(xp_ref[...], g) * keep_prev).astype(BF16)
        h_ref[H + tm:2 * H + tm, :] = (_rms(xn_ref[...], g) * keep_next).astype(BF16)

    h = h_ref[...]

    def conv(w_ref, cw_ref, cb_ref):
        u = jnp.dot(h, w_ref[...], preferred_element_type=F32)
        cw = cw_ref[...]
        return (cw[0:1] * u[H - 1:H - 1 + tm] + cw[1:2] * u[H:H + tm] + cw[2:3] * u[H + 1:H + 1 + tm]
                + cb_ref[...])

    a = conv(wa_ref, cwa_ref, cba_ref)
    gate = conv(wg_ref, cwg_ref, cbg_ref)
    o_ref[...] = (_gelu_tanh(gate) * a).astype(o_ref.dtype)


def up_conv_gate(x, g, w_up, conv_w, conv_b, segs, tm=1024, tn=512):
    M, K = x.shape
    F = w_up.shape[1] // 2
    H = BF16_SUBLANES
    tm = _pick(math.gcd(*[S for _, _, S in segs]), tm, H)
    tn = _pick(F, tn, 128)
    assert all(row0 % tm == 0 for row0, _, _ in segs)
    nj = F // tn
    hb = tm // H
    firsts = tuple((row0 + b * S) // tm for row0, nb, S in segs for b in range(nb))
    lasts = tuple((row0 + (b + 1) * S) // tm - 1 for row0, nb, S in segs for b in range(nb))
    return pl.pallas_call(
        functools.partial(_up_conv_gate_kernel, tm=tm, firsts=firsts, lasts=lasts),
        grid=(M // tm, nj),
        in_specs=[
            pl.BlockSpec((tm, K), lambda i, j: (i, 0)),
            pl.BlockSpec((H, K), lambda i, j: (jnp.maximum(i * hb - 1, 0), 0)),
            pl.BlockSpec((H, K), lambda i, j: (jnp.minimum((i + 1) * hb, M // H - 1), 0)),
            pl.BlockSpec((1, K), lambda i, j: (0, 0)),
            pl.BlockSpec((K, tn), lambda i, j: (0, j)),
            pl.BlockSpec((K, tn), lambda i, j: (0, nj + j)),
            pl.BlockSpec((3, tn), lambda i, j: (0, j)),
            pl.BlockSpec((3, tn), lambda i, j: (0, nj + j)),
            pl.BlockSpec((1, tn), lambda i, j: (0, j)),
            pl.BlockSpec((1, tn), lambda i, j: (0, nj + j)),
        ],
        out_specs=pl.BlockSpec((tm, tn), lambda i, j: (i, j)),
        out_shape=jax.ShapeDtypeStruct((M, F), BF16),
        scratch_shapes=[pltpu.VMEM((tm + 2 * H, K), BF16)],
        compiler_params=_cparams("parallel", "arbitrary"),
        name="up_conv_gate",
    )(x, x, x, g.reshape(1, K), w_up, w_up, conv_w, conv_w, conv_b.reshape(1, 2 * F), conv_b.reshape(1, 2 * F))


def kernel(x_prompt, x_sample, mem_prompt, mem_sample, norm_g, rel_bias, w_in_even, w_out_even,
           diff_lambda, diff_norm_g, w_in_odd, gla_gate_down, gla_gate_up, gla_gate_bias, gla_norm_g,
           w_out_odd, w_xq, w_xkv, w_xo, w_up, conv_w, conv_b, w_down):
    bp, sp, D = x_prompt.shape
    bs, ss, _ = x_sample.shape
    n_mem = mem_prompt.shape[1]
    segs = ((0, bp, sp), (bp * sp, bs, ss))
    M = bp * sp + bs * ss
    x = jnp.concatenate([x_prompt.reshape(bp * sp, D), x_sample.reshape(bs * ss, D)], axis=0)
    mem = jnp.concatenate([mem_prompt.reshape(bp * n_mem, D), mem_sample.reshape(bs * n_mem, D)], axis=0)
    mem_b0 = (0, bp)
    norm_g = norm_g.astype(F32)

    tk = min(DIFF_TK, sp, ss)
    bias = bias_tiles(rel_bias, min(DIFF_TQ, tk), tk)
    gd = FNET_GROUP_DIM
    c_ch, s_ch = _dft_tables(gd, gd ** -0.5)
    w_ch = jnp.concatenate([c_ch, -s_ch], axis=1).astype(BF16)
    fcat = {}
    for _, _, S in segs:
        if S not in fcat:
            c_s, s_s = _dft_tables(S, S ** -0.5)
            fcat[S] = jnp.concatenate([c_s, s_s], axis=1).astype(BF16)
    q_cols = (jnp.arange(w_in_even.shape[2]) >= FNET_WIDTH) & (jnp.arange(w_in_even.shape[2]) < FNET_WIDTH + DIFF_QK)
    col_scale = jnp.where(q_cols, DIFF_HEAD_DIM ** -0.5 * LOG2E, 1.0).astype(F32)

    for l in range(DEPTH):
        ng = norm_g[l]
        i = l // 2
        if l % 2 == 0:
            lam_init = 0.8 - 0.6 * math.exp(-0.3 * l)
            z = prenorm_matmul(x, ng[0], (w_in_even[i] * col_scale).astype(BF16), BF16)
            y = fnet_channel_dft(z, w_ch)
            mixed = None
            mixed_shape = jax.ShapeDtypeStruct((M, FNET_WIDTH + DIFF_WIDTH), BF16)
            for row0, nb, S in segs:
                mixed = fnet_seq_dft(fcat[S], y, mixed, mixed_shape, row0, nb, S)
            for row0, nb, S in segs:
                mixed = diff_attention(z, bias, diff_lambda[i], diff_norm_g[i], mixed, row0, nb, S, lam_init)
            x = matmul_postnorm_residual(mixed, w_out_even[i].astype(BF16), ng[1], x)
        else:
            z = prenorm_matmul(x, ng[0], w_in_odd[i].astype(BF16), BF16)
            w_gd = jnp.concatenate([gla_gate_down[i, 0], gla_gate_down[i, 1]], axis=1)
            w_gd = jnp.pad(w_gd, ((0, 0), (0, GATE_PAD - 2 * GLA_RANK))).astype(BF16)
            lr = prenorm_matmul(x, ng[0], w_gd, F32)
            wgu, bg = [], []
            for d in range(2):
                pad = ((d * GLA_RANK, GATE_PAD - (d + 1) * GLA_RANK), (0, 0))
                wgu.append(jnp.pad(gla_gate_up[i, d], pad).astype(BF16))
                bg.append(gla_gate_bias[i, d].reshape(1, GLA_HK).astype(F32))
            o_f, o_b = gla_bidirectional(z, lr, wgu[0], wgu[1], bg[0], bg[1], segs)
            mixed = gla_post(o_f, o_b, z, gla_norm_g[i])
            x = matmul_postnorm_residual(mixed, w_out_odd[i].astype(BF16), ng[1], x)

        q = prenorm_matmul(x, ng[2], w_xq[l].astype(BF16), BF16)
        kv = prenorm_matmul(mem, ng[4], w_xkv[l].astype(BF16), BF16)
        att = None
        for (row0, nb, S), b0 in zip(segs, mem_b0):
            att = cross_attention(q, kv, att, row0, nb, S, b0, n_mem)
        x = matmul_postnorm_residual(att, w_xo[l].astype(BF16), ng[3], x)

        act = up_conv_gate(x, ng[5], w_up[l].astype(BF16), conv_w[l].astype(F32), conv_b[l].astype(F32), segs)
        x = matmul_postnorm_residual(act, w_down[l].astype(BF16), ng[6], x)

    y_prompt = x[:bp * sp].reshape(bp, sp, D)
    y_sample = x[bp * sp:].reshape(bs, ss, D)
    return (y_prompt, y_sample)
```

```python
import functools
import math

import jax
import jax.numpy as jnp
from jax import lax
from jax.experimental import pallas as pl
from jax.experimental.pallas import tpu as pltpu

F32 = jnp.float32
BF16 = jnp.bfloat16

EPS = 1e-6
DEPTH = 4
LOG2E = math.log2(math.e)
FNET_GROUPS = 4
FNET_GROUP_DIM = 256
FNET_WIDTH = FNET_GROUPS * FNET_GROUP_DIM
DIFF_HEADS = 8
DIFF_HEAD_DIM = 64
DIFF_V_DIM = 2 * DIFF_HEAD_DIM
DIFF_QK = DIFF_HEADS * 2 * DIFF_HEAD_DIM
DIFF_WIDTH = DIFF_HEADS * DIFF_V_DIM
N_BUCKETS = 32
MAX_DISTANCE = 128
DIFF_TQ = 512
DIFF_TK = 1024
GLA_HEADS = 4
GLA_DK = 256
GLA_DV = 512
GLA_RANK = 16
GLA_TAU = 16.0
GLA_HK = GLA_HEADS * GLA_DK
GLA_HV = GLA_HEADS * GLA_DV
GLA_CHUNK = 64
GLA_SUB = 16
GLA_ROWS = 256
GATE_PAD = 128
X_HEADS = 4
X_HEAD_DIM = 512
VMEM_LIMIT_BYTES = 52 * 1024 * 1024
BF16_SUBLANES = 16


def _cparams(*sem):
    return pltpu.CompilerParams(dimension_semantics=sem, vmem_limit_bytes=VMEM_LIMIT_BYTES)


def _pick(n, pref, mult=8):
    if n <= pref:
        return n
    for t in range(pref, 0, -1):
        if n % t == 0 and t % mult == 0:
            return t
    return n


def _rms(x, g):
    return x * lax.rsqrt(jnp.mean(x * x, axis=-1, keepdims=True) + EPS) * g


def _call_into(body, out, args, in_specs, **kw):
    if out is None:
        return pl.pallas_call(body, in_specs=in_specs, **kw)(*args)
    n = len(args)

    def aliased(*refs):
        body(*refs[:n], *refs[n + 1:])

    return pl.pallas_call(aliased, in_specs=list(in_specs) + [pl.BlockSpec(memory_space=pl.ANY)],
                          input_output_aliases={n: 0}, **kw)(*args, out)


def _prenorm_mm_kernel(x_ref, g_ref, w_ref, o_ref, xn_ref):
    @pl.when(pl.program_id(1) == 0)
    def _():
        xn_ref[...] = _rms(x_ref[...], g_ref[...]).astype(BF16)

    o_ref[...] = jnp.dot(xn_ref[...], w_ref[...], preferred_element_type=F32).astype(o_ref.dtype)


def prenorm_matmul(x, g, w, out_dtype, tm=1024, tn=1024):
    M, K = x.shape
    N = w.shape[1]
    tm = _pick(M, tm)
    tn = _pick(N, tn, 128)
    return pl.pallas_call(
        _prenorm_mm_kernel,
        grid=(M // tm, N // tn),
        in_specs=[
            pl.BlockSpec((tm, K), lambda i, j: (i, 0)),
            pl.BlockSpec((1, K), lambda i, j: (0, 0)),
            pl.BlockSpec((K, tn), lambda i, j: (0, j)),
        ],
        out_specs=pl.BlockSpec((tm, tn), lambda i, j: (i, j)),
        out_shape=jax.ShapeDtypeStruct((M, N), out_dtype),
        scratch_shapes=[pltpu.VMEM((tm, K), BF16)],
        compiler_params=_cparams("parallel", "arbitrary"),
        name="prenorm_mm",
    )(x, g.reshape(1, K), w)


def _mm_postnorm_res_kernel(a_ref, w_ref, g_ref, x_ref, o_ref, acc_ref, *, nk):
    k = pl.program_id(1)

    @pl.when(k == 0)
    def _():
        acc_ref[...] = jnp.zeros_like(acc_ref)

    acc_ref[...] += jnp.dot(a_ref[...], w_ref[...], preferred_element_type=F32)

    @pl.when(k == nk - 1)
    def _():
        o_ref[...] = x_ref[...] + _rms(acc_ref[...], g_ref[...])


def _mm_postnorm_res_fullk_kernel(a_ref, w_ref, g_ref, x_ref, o_ref):
    y = jnp.dot(a_ref[...], w_ref[...], preferred_element_type=F32)
    o_ref[...] = x_ref[...] + _rms(y, g_ref[...])


def matmul_postnorm_residual(a, w, g, x, tm=512, tk=2048):
    M, K = a.shape
    D = w.shape[1]
    tm = _pick(M, tm)
    tk = _pick(K, tk, 128)
    nk = K // tk
    if nk == 1:
        return pl.pallas_call(
            _mm_postnorm_res_fullk_kernel,
            grid=(M // tm,),
            in_specs=[
                pl.BlockSpec((tm, K), lambda i: (i, 0)),
                pl.BlockSpec((K, D), lambda i: (0, 0)),
                pl.BlockSpec((1, D), lambda i: (0, 0)),
                pl.BlockSpec((tm, D), lambda i: (i, 0)),
            ],
            out_specs=pl.BlockSpec((tm, D), lambda i: (i, 0)),
            out_shape=jax.ShapeDtypeStruct((M, D), F32),
            compiler_params=_cparams("parallel"),
            name="mm_postnorm_res_fullk",
        )(a, w, g.reshape(1, D), x)
    tm = _pick(M, 2 * tm)
    tk = _pick(K, tk // 4, 128)
    nk = K // tk
    return pl.pallas_call(
        functools.partial(_mm_postnorm_res_kernel, nk=nk),
        grid=(M // tm, nk),
        in_specs=[
            pl.BlockSpec((tm, tk), lambda i, k: (i, k)),
            pl.BlockSpec((tk, D), lambda i, k: (k, 0)),
            pl.BlockSpec((1, D), lambda i, k: (0, 0)),
            pl.BlockSpec((tm, D), lambda i, k: (i, 0)),
        ],
        out_specs=pl.BlockSpec((tm, D), lambda i, k: (i, 0)),
        out_shape=jax.ShapeDtypeStruct((M, D), F32),
        scratch_shapes=[pltpu.VMEM((tm, D), F32)],
        compiler_params=_cparams("parallel", "arbitrary"),
        name="mm_postnorm_res",
    )(a, w, g.reshape(1, D), x)


def _dft_tables(n, scale):
    split = 1
    while split * split < n:
        split *= 2
    hi = n // split
    k = jnp.arange(n, dtype=jnp.int32)[None, :]
    a = jnp.arange(hi, dtype=jnp.int32)[:, None] * split
    b = jnp.arange(split, dtype=jnp.int32)[:, None]
    ang_a = ((a * k) % n).astype(F32) * (2.0 * math.pi / n)
    ang_b = ((b * k) % n).astype(F32) * (2.0 * math.pi / n)
    ca, sa = jnp.cos(ang_a)[:, None, :], jnp.sin(ang_a)[:, None, :]
    cb, sb = jnp.cos(ang_b)[None, :, :], jnp.sin(ang_b)[None, :, :]
    c = (ca * cb - sa * sb) * scale
    s = (sa * cb + ca * sb) * scale
    return c.reshape(n, n), s.reshape(n, n)


def _fnet_channel_kernel(u_ref, w_ref, o_ref):
    y = jnp.dot(u_ref[...], w_ref[...], preferred_element_type=F32)
    o_ref[0] = y[:, :FNET_GROUP_DIM].astype(BF16)
    o_ref[1] = y[:, FNET_GROUP_DIM:].astype(BF16)


def fnet_channel_dft(z, w_ch, tm=1024):
    M = z.shape[0]
    tm = _pick(M, tm)
    gd = FNET_GROUP_DIM
    return pl.pallas_call(
        _fnet_channel_kernel,
        grid=(M // tm, FNET_GROUPS),
        in_specs=[
            pl.BlockSpec((tm, gd), lambda i, g: (i, g)),
            pl.BlockSpec((gd, 2 * gd), lambda i, g: (0, 0)),
        ],
        out_specs=pl.BlockSpec((2, tm, gd), lambda i, g: (0, i, g)),
        out_shape=jax.ShapeDtypeStruct((2, M, FNET_WIDTH), BF16),
        compiler_params=_cparams("parallel", "parallel"),
        name="fnet_channel",
    )(z, w_ch)


def _fnet_seq_kernel(f_ref, y_ref, o_ref, acc_ref, *, nk):
    k = pl.program_id(2)

    @pl.when(k == 0)
    def _():
        acc_ref[...] = jnp.zeros_like(acc_ref)

    acc_ref[...] += jnp.dot(f_ref[...], y_ref[...], preferred_element_type=F32)

    @pl.when(k == nk - 1)
    def _():
        o_ref[...] = acc_ref[...].astype(o_ref.dtype)


def fnet_seq_dft(fcat, y, out, out_shape, row0, nb, S, tm=1024, tk=512):
    tm = _pick(S, tm)
    tk = _pick(S, tk, 128)
    nks = S // tk
    nk = 2 * nks
    C = y.shape[2]
    rb_k = row0 // tk
    rb_m = row0 // tm
    return _call_into(
        functools.partial(_fnet_seq_kernel, nk=nk), out, (fcat, y),
        in_specs=[
            pl.BlockSpec((tm, tk), lambda b, i, k: (i, k)),
            pl.BlockSpec((None, tk, C), lambda b, i, k: (k // nks, rb_k + b * nks + k % nks, 0)),
        ],
        grid=(nb, S // tm, nk),
        out_specs=pl.BlockSpec((tm, C), lambda b, i, k: (rb_m + b * (S // tm) + i, 0)),
        out_shape=out_shape,
        scratch_shapes=[pltpu.VMEM((tm, C), F32)],
        compiler_params=_cparams("parallel", "parallel", "arbitrary"),
        name="fnet_seq",
    )


def _t5_bucket(rel):
    n = N_BUCKETS // 2
    max_exact = n // 2
    base = jnp.where(rel > 0, n, 0)
    a = jnp.abs(rel)
    af = jnp.maximum(a, 1).astype(F32)
    large = max_exact + (jnp.log(af / max_exact) / math.log(MAX_DISTANCE / max_exact)
                         * (n - max_exact)).astype(jnp.int32)
    large = jnp.minimum(large, n - 1)
    return base + jnp.where(a < max_exact, a, large)


def _bias_tile_kernel(rb_ref, bucket_ref, o_ref):
    h = pl.program_id(1)
    bucket = bucket_ref[...]
    acc = jnp.zeros(bucket.shape, F32)
    for n in range(N_BUCKETS):
        acc = jnp.where(bucket == n, rb_ref[n, h], acc)
    o_ref[...] = acc * LOG2E


def bias_tiles(rel_bias, tq, tk):
    r = tk // tq
    assert tq >= MAX_DISTANCE and tk % tq == 0
    n = 2 * r + 2
    d = jnp.arange(-r - 1, r + 1, dtype=jnp.int32)[:, None, None] * tq
    qq = jnp.arange(tq, dtype=jnp.int32)[None, :, None]
    kk = jnp.arange(tk, dtype=jnp.int32)[None, None, :]
    bucket = _t5_bucket(d + kk - qq)
    return pl.pallas_call(
        _bias_tile_kernel,
        grid=(n, DIFF_HEADS),
        in_specs=[
            pl.BlockSpec(memory_space=pltpu.SMEM),
            pl.BlockSpec((None, tq, tk), lambda t, h: (t, 0, 0)),
        ],
        out_specs=pl.BlockSpec((None, None, tq, tk), lambda t, h: (t, h, 0, 0)),
        out_shape=jax.ShapeDtypeStruct((n, DIFF_HEADS, tq, tk), F32),
        compiler_params=_cparams("parallel", "parallel"),
        name="bias_tiles",
    )(rel_bias.astype(F32), bucket)


def _diff_attn_kernel(q_ref, k_ref, v_ref, bias_ref, lam_ref, hg_ref, o_ref, m_ref, acc_ref,
                      *, nkc, tq, tk, lam_init):
    i = pl.program_id(2)
    r = tk // tq
    vd = DIFF_V_DIM
    m_ref[...] = jnp.full_like(m_ref, -jnp.inf)
    acc_ref[...] = jnp.zeros_like(acc_ref)
    ones = jnp.ones((tk, vd), BF16)

    q = q_ref[...]
    lane = lax.broadcasted_iota(jnp.int32, q.shape, 1)
    zero = jnp.zeros_like(q)
    halves = (jnp.where(lane < DIFF_HEAD_DIM, q, zero), jnp.where(lane >= DIFF_HEAD_DIM, q, zero))

    def chunk(j, carry):
        rows = pl.ds(pl.multiple_of(j * tk, tk), tk)
        k = k_ref[rows, :]
        vx = jnp.concatenate([v_ref[rows, :], ones], axis=1)
        bias = bias_ref[jnp.clip(j * r - i, -r - 1, r) + r + 1]
        for c, qc in enumerate(halves):
            s = lax.dot_general(qc, k, (((1,), (1,)), ((), ())), preferred_element_type=F32) + bias
            m_prev = m_ref[c]
            m_new = jnp.maximum(m_prev, jnp.max(s, axis=-1, keepdims=True))
            alpha = jnp.exp2(m_prev - m_new)
            p = jnp.exp2(s - m_new).astype(BF16)
            acc_ref[c] = alpha * acc_ref[c] + jnp.dot(p, vx, preferred_element_type=F32)
            m_ref[c] = m_new
        return carry

    lax.fori_loop(0, nkc, chunk, 0)

    lp = lam_ref[...]
    lam = (jnp.exp(jnp.sum(lp[0:1] * lp[1:2], axis=-1, keepdims=True))
           - jnp.exp(jnp.sum(lp[2:3] * lp[3:4], axis=-1, keepdims=True)) + lam_init)
    a0 = acc_ref[0]
    a1 = acc_ref[1]
    o = a0[:, :vd] / a0[:, vd:] - lam * (a1[:, :vd] / a1[:, vd:])
    o_ref[...] = (_rms(o, hg_ref[...]) * (1.0 - lam_init)).astype(o_ref.dtype)


def diff_attention(z, bias, lam_p, head_g, out, row0, nb, S, lam_init):
    tq, tk = bias.shape[2], bias.shape[3]
    assert S % tk == 0 and row0 % S == 0
    nq = S // tq
    rb = row0 // tq
    sb = row0 // S
    vd = DIFF_V_DIM
    q0 = FNET_WIDTH // vd
    k0 = (FNET_WIDTH + DIFF_QK) // vd
    v0 = (FNET_WIDTH + 2 * DIFF_QK) // vd
    o0 = FNET_WIDTH // vd
    return _call_into(
        functools.partial(_diff_attn_kernel, nkc=S // tk, tq=tq, tk=tk, lam_init=lam_init), out,
        (z, z, z, bias, lam_p.astype(F32), head_g.reshape(1, vd).astype(F32)),
        in_specs=[
            pl.BlockSpec((tq, vd), lambda b, h, i: (rb + b * nq + i, q0 + h)),
            pl.BlockSpec((S, vd), lambda b, h, i: (sb + b, k0 + h)),
            pl.BlockSpec((S, vd), lambda b, h, i: (sb + b, v0 + h)),
            pl.BlockSpec((bias.shape[0], None, tq, tk), lambda b, h, i: (0, h, 0, 0)),
            pl.BlockSpec((4, DIFF_HEAD_DIM), lambda b, h, i: (0, 0)),
            pl.BlockSpec((1, vd), lambda b, h, i: (0, 0)),
        ],
        grid=(nb, DIFF_HEADS, nq),
        out_specs=pl.BlockSpec((tq, vd), lambda b, h, i: (rb + b * nq + i, o0 + h)),
        out_shape=jax.ShapeDtypeStruct(out.shape, out.dtype),
        scratch_shapes=[
            pltpu.VMEM((2, tq, 1), F32),
            pltpu.VMEM((2, tq, 2 * vd), F32),
        ],
        compiler_params=_cparams("parallel", "parallel", "arbitrary"),
        name="diff_attn",
    )


def _log_sigmoid(x):
    return jnp.minimum(x, 0.0) - jnp.log(1.0 + jnp.exp(-jnp.abs(x)))


def _gla_chunk(q, k, v, pre, st, rev):
    C = GLA_CHUNK
    nt = (((1,), (1,)), ((), ()))
    g = _log_sigmoid(pre) * (1.0 / GLA_TAU)
    row = lax.broadcasted_iota(jnp.int32, (C, C), 0)
    col = lax.broadcasted_iota(jnp.int32, (C, C), 1)
    tri = jnp.where((row <= col) if rev else (row >= col), 1.0, 0.0).astype(BF16)
    g_hi = g.astype(BF16)
    g_lo = (g - g_hi.astype(F32)).astype(BF16)
    b = (jnp.dot(tri, g_hi, preferred_element_type=F32)
         + jnp.dot(tri, g_lo, preferred_element_type=F32))
    edge = 0 if rev else C - 1
    b_edge = b[edge:edge + 1, :]

    qe = (q * jnp.exp(b)).astype(BF16)
    inter = lax.dot_general(qe, st.astype(BF16), nt, preferred_element_type=F32)

    sub = GLA_SUB
    lane = lax.broadcasted_iota(jnp.int32, (sub, C), 1)
    trow = lax.broadcasted_iota(jnp.int32, (sub, 1), 0)
    pieces = []
    for j in range(C // sub):
        lo = j * sub
        hi = lo + sub
        qj = q[lo:hi]
        bj = b[lo:hi]
        aj = jnp.zeros((sub, C), F32)
        for s in range(sub):
            ks = k[lo + s:lo + s + 1]
            bs = b[lo + s:lo + s + 1]
            seen = (trow <= s) if rev else (trow >= s)
            e = jnp.exp(jnp.where(seen, bj - bs, -jnp.inf))
            colv = jnp.sum(qj * ks * e, axis=-1, keepdims=True)
            aj = aj + jnp.where(lane == lo + s, colv, 0.0)
        if (hi < C) if rev else (j > 0):
            mj = b[hi:hi + 1] if rev else b[lo - 1:lo]
            qt = (qj * jnp.exp(bj - mj)).astype(BF16)
            kt = (k * jnp.exp(jnp.minimum(mj - b, 0.0))).astype(BF16)
            off = lax.dot_general(qt, kt, nt, preferred_element_type=F32)
            aj = aj + jnp.where((lane >= hi) if rev else (lane < lo), off, 0.0)
        pieces.append(aj)
    attn = jnp.concatenate(pieces, axis=0)
    intra = jnp.dot(attn.astype(BF16), v, preferred_element_type=F32)

    kd = (k * jnp.exp(b_edge - b)).astype(BF16)
    vt = v.astype(F32).T.astype(BF16)
    st_new = st * jnp.exp(b_edge) + jnp.dot(vt, kd, preferred_element_type=F32)
    return inter + intra, st_new


def _gla_bidir_kernel(qf_ref, kf_ref, vf_ref, lrf_ref, qr_ref, kr_ref, vr_ref, lrr_ref,
                      wguf_ref, wgur_ref, bgf_ref, bgr_ref, of_ref, or_ref, stf_ref, str_ref,
                      *, rows, resets_f, resets_r):
    t = pl.program_id(1)

    def any_of(blocks):
        return functools.reduce(jnp.logical_or, [t == b for b in blocks])

    @pl.when(any_of(resets_f))
    def _():
        stf_ref[...] = jnp.zeros_like(stf_ref)

    @pl.when(any_of(resets_r))
    def _():
        str_ref[...] = jnp.zeros_like(str_ref)

    nc = rows // GLA_CHUNK
    sides = ((qf_ref, kf_ref, vf_ref, lrf_ref, wguf_ref[...], bgf_ref[...], of_ref, stf_ref, False),
             (qr_ref, kr_ref, vr_ref, lrr_ref, wgur_ref[...], bgr_ref[...], or_ref, str_ref, True))

    def step(c, carry):
        for q_ref, k_ref, v_ref, lr_ref, wgu, bg, o_ref, st_ref, rev in sides:
            cc = nc - 1 - c if rev else c
            sl = pl.ds(pl.multiple_of(cc * GLA_CHUNK, GLA_CHUNK), GLA_CHUNK)
            q = q_ref[sl, :].astype(F32) * (GLA_DK ** -0.5)
            k = k_ref[sl, :].astype(F32)
            v = v_ref[sl, :]
            pre = jnp.dot(lr_ref[sl, :].astype(BF16), wgu, preferred_element_type=F32) + bg
            out, st_new = _gla_chunk(q, k, v, pre, st_ref[...], rev)
            o_ref[sl, :] = out
            st_ref[...] = st_new
        return carry

    lax.fori_loop(0, nc, step, 0, unroll=True)


def gla_bidirectional(z, lr, wgu_f, wgu_r, bg_f, bg_r, segs):
    M = z.shape[0]
    rows = GLA_ROWS
    assert all(row0 % rows == 0 and S % rows == 0 for row0, _, S in segs)
    T = M // rows
    starts = [(row0 + b * S) // rows for row0, nb, S in segs for b in range(nb)]
    lasts = [(row0 + (b + 1) * S) // rows - 1 for row0, nb, S in segs for b in range(nb)]
    resets_f = tuple(starts)
    resets_r = tuple(T - 1 - b for b in lasts)
    kb = GLA_HK // GLA_DK
    vb = 2 * GLA_HK // GLA_DV

    def side(rblk):
        return [
            pl.BlockSpec((rows, GLA_DK), lambda h, t: (rblk(t), h)),
            pl.BlockSpec((rows, GLA_DK), lambda h, t: (rblk(t), kb + h)),
            pl.BlockSpec((rows, GLA_DV), lambda h, t: (rblk(t), vb + h)),
            pl.BlockSpec((rows, GATE_PAD), lambda h, t: (rblk(t), 0)),
        ]

    fwd = lambda t: t
    rev = lambda t: T - 1 - t
    wspec = pl.BlockSpec((GATE_PAD, GLA_DK), lambda h, t: (0, h))
    bspec = pl.BlockSpec((1, GLA_DK), lambda h, t: (0, h))
    return pl.pallas_call(
        functools.partial(_gla_bidir_kernel, rows=rows, resets_f=resets_f, resets_r=resets_r),
        grid=(GLA_HEADS, T),
        in_specs=side(fwd) + side(rev) + [wspec, wspec, bspec, bspec],
        out_specs=[pl.BlockSpec((rows, GLA_DV), lambda h, t: (fwd(t), h)),
                   pl.BlockSpec((rows, GLA_DV), lambda h, t: (rev(t), h))],
        out_shape=[jax.ShapeDtypeStruct((M, GLA_HV), F32), jax.ShapeDtypeStruct((M, GLA_HV), F32)],
        scratch_shapes=[pltpu.VMEM((GLA_DV, GLA_DK), F32), pltpu.VMEM((GLA_DV, GLA_DK), F32)],
        compiler_params=_cparams("parallel", "arbitrary"),
        name="gla_bidir",
    )(z, z, z, lr, z, z, z, lr, wgu_f, wgu_r, bg_f, bg_r)


def _gla_post_kernel(of_ref, ob_ref, r_ref, g_ref, o_ref):
    o = _rms(of_ref[...] + ob_ref[...], g_ref[...])
    r = r_ref[...].astype(F32)
    o_ref[...] = (o * (r / (1.0 + jnp.exp(-r)))).astype(o_ref.dtype)


def gla_post(o_f, o_b, z, head_g, tm=512):
    M = o_f.shape[0]
    tm = _pick(M, tm)
    r0 = (2 * GLA_HK + GLA_HV) // GLA_DV
    return pl.pallas_call(
        _gla_post_kernel,
        grid=(M // tm, GLA_HEADS),
        in_specs=[
            pl.BlockSpec((tm, GLA_DV), lambda i, h: (i, h)),
            pl.BlockSpec((tm, GLA_DV), lambda i, h: (i, h)),
            pl.BlockSpec((tm, GLA_DV), lambda i, h: (i, r0 + h)),
            pl.BlockSpec((1, GLA_DV), lambda i, h: (0, 0)),
        ],
        out_specs=pl.BlockSpec((tm, GLA_DV), lambda i, h: (i, h)),
        out_shape=jax.ShapeDtypeStruct((M, GLA_HV), BF16),
        compiler_params=_cparams("parallel", "parallel"),
        name="gla_post",
    )(o_f, o_b, z, head_g.reshape(1, GLA_DV).astype(F32))


def _cross_attn_kernel(q_ref, k_ref, v_ref, o_ref):
    for h in range(X_HEADS):
        cols = slice(h * X_HEAD_DIM, (h + 1) * X_HEAD_DIM)
        s = lax.dot_general(q_ref[:, cols], k_ref[:, cols], (((1,), (1,)), ((), ())),
                            preferred_element_type=F32) * (X_HEAD_DIM ** -0.5)
        m = jnp.max(s, axis=-1, keepdims=True)
        p = jnp.exp(s - m)
        p = p / jnp.sum(p, axis=-1, keepdims=True)
        o_ref[:, cols] = jnp.dot(p.astype(BF16), v_ref[:, cols],
                                 preferred_element_type=F32).astype(o_ref.dtype)


def cross_attention(q, kv, out, row0, nb, S, mem_b0, n_mem, tq=512):
    tq = _pick(S, tq)
    nq = S // tq
    rb = row0 // tq
    width = X_HEADS * X_HEAD_DIM
    return _call_into(
        _cross_attn_kernel, out, (q, kv, kv),
        in_specs=[
            pl.BlockSpec((tq, width), lambda b, i: (rb + b * nq + i, 0)),
            pl.BlockSpec((n_mem, width), lambda b, i: (mem_b0 + b, 0)),
            pl.BlockSpec((n_mem, width), lambda b, i: (mem_b0 + b, 1)),
        ],
        grid=(nb, nq),
        out_specs=pl.BlockSpec((tq, width), lambda b, i: (rb + b * nq + i, 0)),
        out_shape=jax.ShapeDtypeStruct(q.shape, q.dtype),
        compiler_params=_cparams("parallel", "parallel"),
        name="cross_attn",
    )


def _gelu_tanh(x):
    return 0.5 * x * (1.0 + jnp.tanh(math.sqrt(2.0 / math.pi) * (x + 0.044715 * x * x * x)))


def _up_conv_gate_kernel(x_ref, xp_ref, xn_ref, g_ref, wa_ref, wg_ref, cwa_ref, cwg_ref, cba_ref, cbg_ref,
                         o_ref, h_ref, *, tm, firsts, lasts):
    i = pl.program_id(0)
    H = BF16_SUBLANES

    def any_of(tiles):
        return functools.reduce(jnp.logical_or, [i == t for t in tiles])

    @pl.when(pl.program_id(1) == 0)
    def _():
        g = g_ref[...]
        h_ref[H:H + tm, :] = _rms(x_ref[...], g).astype(BF16)
        keep_prev = jnp.where(any_of(firsts), 0.0, 1.0)
        keep_next = jnp.where(any_of(lasts), 0.0, 1.0)
        h_ref[0:H, :] = (_rms(xp_ref[...], g) * keep_prev).astype(BF16)
        h_ref[H + tm:2 * H + tm, :] = (_rms(xn_ref[...], g) * keep_next).astype(BF16)

    h = h_ref[...]

    def conv(w_ref, cw_ref, cb_ref):
        u = jnp.dot(h, w_ref[...], preferred_element_type=F32)
        cw = cw_ref[...]
        return (cw[0:1] * u[H - 1:H - 1 + tm] + cw[1:2] * u[H:H + tm] + cw[2:3] * u[H + 1:H + 1 + tm]
                + cb_ref[...])

    a = conv(wa_ref, cwa_ref, cba_ref)
    gate = conv(wg_ref, cwg_ref, cbg_ref)
    o_ref[...] = (_gelu_tanh(gate) * a).astype(o_ref.dtype)


def up_conv_gate(x, g, w_up, conv_w, conv_b, segs, tm=1024, tn=512):
    M, K = x.shape
    F = w_up.shape[1] // 2
    H = BF16_SUBLANES
    tm = _pick(math.gcd(*[S for _, _, S in segs]), tm, H)
    tn = _pick(F, tn, 128)
    assert all(row0 % tm == 0 for row0, _, _ in segs)
    nj = F // tn
    hb = tm // H
    firsts = tuple((row0 + b * S) // tm for row0, nb, S in segs for b in range(nb))
    lasts = tuple((row0 + (b + 1) * S) // tm - 1 for row0, nb, S in segs for b in range(nb))
    return pl.pallas_call(
        functools.partial(_up_conv_gate_kernel, tm=tm, firsts=firsts, lasts=lasts),
        grid=(M // tm, nj),
        in_specs=[
            pl.BlockSpec((tm, K), lambda i, j: (i, 0)),
            pl.BlockSpec((H, K), lambda i, j: (jnp.maximum(i * hb - 1, 0), 0)),
            pl.BlockSpec((H, K), lambda i, j: (jnp.minimum((i + 1) * hb, M // H - 1), 0)),
            pl.BlockSpec((1, K), lambda i, j: (0, 0)),
            pl.BlockSpec((K, tn), lambda i, j: (0, j)),
            pl.BlockSpec((K, tn), lambda i, j: (0, nj + j)),
            pl.BlockSpec((3, tn), lambda i, j: (0, j)),
            pl.BlockSpec((3, tn), lambda i, j: (0, nj + j)),
            pl.BlockSpec((1, tn), lambda i, j: (0, j)),
            pl.BlockSpec((1, tn), lambda i, j: (0, nj + j)),
        ],
        out_specs=pl.BlockSpec((tm, tn), lambda i, j: (i, j)),
        out_shape=jax.ShapeDtypeStruct((M, F), BF16),
        scratch_shapes=[pltpu.VMEM((tm + 2 * H, K), BF16)],
        compiler_params=_cparams("parallel", "arbitrary"),
        name="up_conv_gate",
    )(x, x, x, g.reshape(1, K), w_up, w_up, conv_w, conv_w, conv_b.reshape(1, 2 * F), conv_b.reshape(1, 2 * F))


def kernel(x_prompt, x_sample, mem_prompt, mem_sample, norm_g, rel_bias, w_in_even, w_out_even,
           diff_lambda, diff_norm_g, w_in_odd, gla_gate_down, gla_gate_up, gla_gate_bias, gla_norm_g,
           w_out_odd, w_xq, w_xkv, w_xo, w_up, conv_w, conv_b, w_down):
    bp, sp, D = x_prompt.shape
    bs, ss, _ = x_sample.shape
    n_mem = mem_prompt.shape[1]
    segs = ((0, bp, sp), (bp * sp, bs, ss))
    M = bp * sp + bs * ss
    x = jnp.concatenate([x_prompt.reshape(bp * sp, D), x_sample.reshape(bs * ss, D)], axis=0)
    mem = jnp.concatenate([mem_prompt.reshape(bp * n_mem, D), mem_sample.reshape(bs * n_mem, D)], axis=0)
    mem_b0 = (0, bp)
    norm_g = norm_g.astype(F32)

    tk = min(DIFF_TK, sp, ss)
    bias = bias_tiles(rel_bias, min(DIFF_TQ, tk), tk)
    gd = FNET_GROUP_DIM
    c_ch, s_ch = _dft_tables(gd, gd ** -0.5)
    w_ch = jnp.concatenate([c_ch, -s_ch], axis=1).astype(BF16)
    fcat = {}
    for _, _, S in segs:
        if S not in fcat:
            c_s, s_s = _dft_tables(S, S ** -0.5)
            fcat[S] = jnp.concatenate([c_s, s_s], axis=1).astype(BF16)
    q_cols = (jnp.arange(w_in_even.shape[2]) >= FNET_WIDTH) & (jnp.arange(w_in_even.shape[2]) < FNET_WIDTH + DIFF_QK)
    col_scale = jnp.where(q_cols, DIFF_HEAD_DIM ** -0.5 * LOG2E, 1.0).astype(F32)

    for l in range(DEPTH):
        ng = norm_g[l]
        i = l // 2
        if l % 2 == 0:
            lam_init = 0.8 - 0.6 * math.exp(-0.3 * l)
            z = prenorm_matmul(x, ng[0], (w_in_even[i] * col_scale).astype(BF16), BF16)
            y = fnet_channel_dft(z, w_ch)
            mixed = None
            mixed_shape = jax.ShapeDtypeStruct((M, FNET_WIDTH + DIFF_WIDTH), BF16)
            for row0, nb, S in segs:
                mixed = fnet_seq_dft(fcat[S], y, mixed, mixed_shape, row0, nb, S)
            for row0, nb, S in segs:
                mixed = diff_attention(z, bias, diff_lambda[i], diff_norm_g[i], mixed, row0, nb, S, lam_init)
            x = matmul_postnorm_residual(mixed, w_out_even[i].astype(BF16), ng[1], x)
        else:
            z = prenorm_matmul(x, ng[0], w_in_odd[i].astype(BF16), BF16)
            w_gd = jnp.concatenate([gla_gate_down[i, 0], gla_gate_down[i, 1]], axis=1)
            w_gd = jnp.pad(w_gd, ((0, 0), (0, GATE_PAD - 2 * GLA_RANK))).astype(BF16)
            lr = prenorm_matmul(x, ng[0], w_gd, F32)
            wgu, bg = [], []
            for d in range(2):
                pad = ((d * GLA_RANK, GATE_PAD - (d + 1) * GLA_RANK), (0, 0))
                wgu.append(jnp.pad(gla_gate_up[i, d], pad).astype(BF16))
                bg.append(gla_gate_bias[i, d].reshape(1, GLA_HK).astype(F32))
            o_f, o_b = gla_bidirectional(z, lr, wgu[0], wgu[1], bg[0], bg[1], segs)
            mixed = gla_post(o_f, o_b, z, gla_norm_g[i])
            x = matmul_postnorm_residual(mixed, w_out_odd[i].astype(BF16), ng[1], x)

        q = prenorm_matmul(x, ng[2], w_xq[l].astype(BF16), BF16)
        kv = prenorm_matmul(mem, ng[4], w_xkv[l].astype(BF16), BF16)
        att = None
        for (row0, nb, S), b0 in zip(segs, mem_b0):
            att = cross_attention(q, kv, att, row0, nb, S, b0, n_mem)
        x = matmul_postnorm_residual(att, w_xo[l].astype(BF16), ng[3], x)

        act = up_conv_gate(x, ng[5], w_up[l].astype(BF16), conv_w[l].astype(F32), conv_b[l].astype(F32), segs)
        x = matmul_postnorm_residual(act, w_down[l].astype(BF16), ng[6], x)

    y_prompt = x[:bp * sp].reshape(bp, sp, D)
    y_sample = x[bp * sp:].reshape(bs, ss, D)
    return (y_prompt, y_sample)
```

```python
import functools
import math

import jax
import jax.numpy as jnp
from jax import lax
from jax.experimental import pallas as pl
from jax.experimental.pallas import tpu as pltpu

F32 = jnp.float32
BF16 = jnp.bfloat16

EPS = 1e-6
DEPTH = 4
LOG2E = math.log2(math.e)
FNET_GROUPS = 4
FNET_GROUP_DIM = 256
FNET_WIDTH = FNET_GROUPS * FNET_GROUP_DIM
DIFF_HEADS = 8
DIFF_HEAD_DIM = 64
DIFF_V_DIM = 2 * DIFF_HEAD_DIM
DIFF_QK = DIFF_HEADS * 2 * DIFF_HEAD_DIM
DIFF_WIDTH = DIFF_HEADS * DIFF_V_DIM
N_BUCKETS = 32
MAX_DISTANCE = 128
DIFF_TQ = 512
DIFF_TK = 1024
GLA_HEADS = 4
GLA_DK = 256
GLA_DV = 512
GLA_RANK = 16
GLA_TAU = 16.0
GLA_HK = GLA_HEADS * GLA_DK
GLA_HV = GLA_HEADS * GLA_DV
GLA_CHUNK = 64
GLA_SUB = 16
GLA_ROWS = 256
GATE_PAD = 128
X_HEADS = 4
X_HEAD_DIM = 512
VMEM_LIMIT_BYTES = 52 * 1024 * 1024
BF16_SUBLANES = 16


def _cparams(*sem):
    return pltpu.CompilerParams(dimension_semantics=sem, vmem_limit_bytes=VMEM_LIMIT_BYTES)


def _pick(n, pref, mult=8):
    if n <= pref:
        return n
    for t in range(pref, 0, -1):
        if n % t == 0 and t % mult == 0:
            return t
    return n


def _rms(x, g):
    return x * lax.rsqrt(jnp.mean(x * x, axis=-1, keepdims=True) + EPS) * g


def _call_into(body, out, args, in_specs, **kw):
    if out is None:
        return pl.pallas_call(body, in_specs=in_specs, **kw)(*args)
    n = len(args)

    def aliased(*refs):
        body(*refs[:n], *refs[n + 1:])

    return pl.pallas_call(aliased, in_specs=list(in_specs) + [pl.BlockSpec(memory_space=pl.ANY)],
                          input_output_aliases={n: 0}, **kw)(*args, out)


def _stack_rows_kernel(a_ref, b_ref, o_ref, *, na):
    i = pl.program_id(0)

    @pl.when(i < na)
    def _():
        o_ref[...] = a_ref[...]

    @pl.when(i >= na)
    def _():
        o_ref[...] = b_ref[...]


def stack_rows(a, b, tm=1024):
    Ma, D = a.shape
    Mb = b.shape[0]
    tm = _pick(math.gcd(Ma, Mb), tm)
    na, nb = Ma // tm, Mb // tm
    return pl.pallas_call(
        functools.partial(_stack_rows_kernel, na=na),
        grid=(na + nb,),
        in_specs=[
            pl.BlockSpec((tm, D), lambda i: (jnp.minimum(i, na - 1), 0)),
            pl.BlockSpec((tm, D), lambda i: (jnp.maximum(i - na, 0), 0)),
        ],
        out_specs=pl.BlockSpec((tm, D), lambda i: (i, 0)),
        out_shape=jax.ShapeDtypeStruct((Ma + Mb, D), a.dtype),
        compiler_params=_cparams("arbitrary"),
        name="stack_rows",
    )(a, b)


def _prenorm_mm_kernel(x_ref, g_ref, w_ref, o_ref, xn_ref):
    @pl.when(pl.program_id(1) == 0)
    def _():
        xn_ref[...] = _rms(x_ref[...], g_ref[...]).astype(BF16)

    o_ref[...] = jnp.dot(xn_ref[...], w_ref[...], preferred_element_type=F32).astype(o_ref.dtype)


def prenorm_matmul(x, g, w, out_dtype, tm=1024, tn=1024):
    M, K = x.shape
    N = w.shape[1]
    tm = _pick(M, tm)
    tn = _pick(N, tn, 128)
    return pl.pallas_call(
        _prenorm_mm_kernel,
        grid=(M // tm, N // tn),
        in_specs=[
            pl.BlockSpec((tm, K), lambda i, j: (i, 0)),
            pl.BlockSpec((1, K), lambda i, j: (0, 0)),
            pl.BlockSpec((K, tn), lambda i, j: (0, j)),
        ],
        out_specs=pl.BlockSpec((tm, tn), lambda i, j: (i, j)),
        out_shape=jax.ShapeDtypeStruct((M, N), out_dtype),
        scratch_shapes=[pltpu.VMEM((tm, K), BF16)],
        compiler_params=_cparams("parallel", "arbitrary"),
        name="prenorm_mm",
    )(x, g.reshape(1, K), w)


def _mm_postnorm_res_kernel(a_ref, w_ref, g_ref, x_ref, o_ref, acc_ref, *, nk):
    k = pl.program_id(1)

    @pl.when(k == 0)
    def _():
        acc_ref[...] = jnp.zeros_like(acc_ref)

    acc_ref[...] += jnp.dot(a_ref[...], w_ref[...], preferred_element_type=F32)

    @pl.when(k == nk - 1)
    def _():
        o_ref[...] = x_ref[...] + _rms(acc_ref[...], g_ref[...])


def _mm_postnorm_res_fullk_kernel(a_ref, w_ref, g_ref, x_ref, o_ref):
    y = jnp.dot(a_ref[...], w_ref[...], preferred_element_type=F32)
    o_ref[...] = x_ref[...] + _rms(y, g_ref[...])


def matmul_postnorm_residual(a, w, g, x, tm=512, tk=2048):
    M, K = a.shape
    D = w.shape[1]
    tm = _pick(M, tm)
    tk = _pick(K, tk, 128)
    nk = K // tk
    if nk == 1:
        return pl.pallas_call(
            _mm_postnorm_res_fullk_kernel,
            grid=(M // tm,),
            in_specs=[
                pl.BlockSpec((tm, K), lambda i: (i, 0)),
                pl.BlockSpec((K, D), lambda i: (0, 0)),
                pl.BlockSpec((1, D), lambda i: (0, 0)),
                pl.BlockSpec((tm, D), lambda i: (i, 0)),
            ],
            out_specs=pl.BlockSpec((tm, D), lambda i: (i, 0)),
            out_shape=jax.ShapeDtypeStruct((M, D), F32),
            compiler_params=_cparams("parallel"),
            name="mm_postnorm_res_fullk",
        )(a, w, g.reshape(1, D), x)
    return pl.pallas_call(
        functools.partial(_mm_postnorm_res_kernel, nk=nk),
        grid=(M // tm, nk),
        in_specs=[
            pl.BlockSpec((tm, tk), lambda i, k: (i, k)),
            pl.BlockSpec((tk, D), lambda i, k: (k, 0)),
            pl.BlockSpec((1, D), lambda i, k: (0, 0)),
            pl.BlockSpec((tm, D), lambda i, k: (i, 0)),
        ],
        out_specs=pl.BlockSpec((tm, D), lambda i, k: (i, 0)),
        out_shape=jax.ShapeDtypeStruct((M, D), F32),
        scratch_shapes=[pltpu.VMEM((tm, D), F32)],
        compiler_params=_cparams("parallel", "arbitrary"),
        name="mm_postnorm_res",
    )(a, w, g.reshape(1, D), x)


def _dft_tables(n, scale):
    split = 1
    while split * split < n:
        split *= 2
    hi = n // split
    k = jnp.arange(n, dtype=jnp.int32)[None, :]
    a = jnp.arange(hi, dtype=jnp.int32)[:, None] * split
    b = jnp.arange(split, dtype=jnp.int32)[:, None]
    ang_a = ((a * k) % n).astype(F32) * (2.0 * math.pi / n)
    ang_b = ((b * k) % n).astype(F32) * (2.0 * math.pi / n)
    ca, sa = jnp.cos(ang_a)[:, None, :], jnp.sin(ang_a)[:, None, :]
    cb, sb = jnp.cos(ang_b)[None, :, :], jnp.sin(ang_b)[None, :, :]
    c = (ca * cb - sa * sb) * scale
    s = (sa * cb + ca * sb) * scale
    return c.reshape(n, n), s.reshape(n, n)


def _fnet_channel_kernel(u_ref, w_ref, o_ref):
    y = jnp.dot(u_ref[...], w_ref[...], preferred_element_type=F32)
    o_ref[0] = y[:, :FNET_GROUP_DIM].astype(BF16)
    o_ref[1] = y[:, FNET_GROUP_DIM:].astype(BF16)


def fnet_channel_dft(z, w_ch, tm=1024):
    M = z.shape[0]
    tm = _pick(M, tm)
    gd = FNET_GROUP_DIM
    return pl.pallas_call(
        _fnet_channel_kernel,
        grid=(M // tm, FNET_GROUPS),
        in_specs=[
            pl.BlockSpec((tm, gd), lambda i, g: (i, g)),
            pl.BlockSpec((gd, 2 * gd), lambda i, g: (0, 0)),
        ],
        out_specs=pl.BlockSpec((2, tm, gd), lambda i, g: (0, i, g)),
        out_shape=jax.ShapeDtypeStruct((2, M, FNET_WIDTH), BF16),
        compiler_params=_cparams("parallel", "parallel"),
        name="fnet_channel",
    )(z, w_ch)


def _fnet_seq_kernel(f_ref, y_ref, o_ref, acc_ref, *, nk):
    k = pl.program_id(2)

    @pl.when(k == 0)
    def _():
        acc_ref[...] = jnp.zeros_like(acc_ref)

    acc_ref[...] += jnp.dot(f_ref[...], y_ref[...], preferred_element_type=F32)

    @pl.when(k == nk - 1)
    def _():
        o_ref[...] = acc_ref[...].astype(o_ref.dtype)


def fnet_seq_dft(fcat, y, out, out_shape, row0, nb, S, tm=1024, tk=512):
    tm = _pick(S, tm)
    tk = _pick(S, tk, 128)
    nks = S // tk
    nk = 2 * nks
    C = y.shape[2]
    rb_k = row0 // tk
    rb_m = row0 // tm
    return _call_into(
        functools.partial(_fnet_seq_kernel, nk=nk), out, (fcat, y),
        in_specs=[
            pl.BlockSpec((tm, tk), lambda b, i, k: (i, k)),
            pl.BlockSpec((None, tk, C), lambda b, i, k: (k // nks, rb_k + b * nks + k % nks, 0)),
        ],
        grid=(nb, S // tm, nk),
        out_specs=pl.BlockSpec((tm, C), lambda b, i, k: (rb_m + b * (S // tm) + i, 0)),
        out_shape=out_shape,
        scratch_shapes=[pltpu.VMEM((tm, C), F32)],
        compiler_params=_cparams("parallel", "parallel", "arbitrary"),
        name="fnet_seq",
    )


def _t5_bucket(rel):
    n = N_BUCKETS // 2
    max_exact = n // 2
    base = jnp.where(rel > 0, n, 0)
    a = jnp.abs(rel)
    af = jnp.maximum(a, 1).astype(F32)
    large = max_exact + (jnp.log(af / max_exact) / math.log(MAX_DISTANCE / max_exact)
                         * (n - max_exact)).astype(jnp.int32)
    large = jnp.minimum(large, n - 1)
    return base + jnp.where(a < max_exact, a, large)


def _bias_tile_kernel(rb_ref, bucket_ref, o_ref):
    h = pl.program_id(1)
    bucket = bucket_ref[...]
    acc = jnp.zeros(bucket.shape, F32)
    for n in range(N_BUCKETS):
        acc = jnp.where(bucket == n, rb_ref[n, h], acc)
    o_ref[...] = acc * LOG2E


def bias_tiles(rel_bias, tq, tk):
    r = tk // tq
    assert tq >= MAX_DISTANCE and tk % tq == 0
    n = 2 * r + 2
    d = jnp.arange(-r - 1, r + 1, dtype=jnp.int32)[:, None, None] * tq
    qq = jnp.arange(tq, dtype=jnp.int32)[None, :, None]
    kk = jnp.arange(tk, dtype=jnp.int32)[None, None, :]
    bucket = _t5_bucket(d + kk - qq)
    return pl.pallas_call(
        _bias_tile_kernel,
        grid=(n, DIFF_HEADS),
        in_specs=[
            pl.BlockSpec(memory_space=pltpu.SMEM),
            pl.BlockSpec((None, tq, tk), lambda t, h: (t, 0, 0)),
        ],
        out_specs=pl.BlockSpec((None, None, tq, tk), lambda t, h: (t, h, 0, 0)),
        out_shape=jax.ShapeDtypeStruct((n, DIFF_HEADS, tq, tk), F32),
        compiler_params=_cparams("parallel", "parallel"),
        name="bias_tiles",
    )(rel_bias.astype(F32), bucket)


def _diff_attn_kernel(q_ref, k_ref, v_ref, bias_ref, lam_ref, hg_ref, o_ref, m_ref, acc_ref,
                      *, nkc, tq, tk, lam_init):
    i = pl.program_id(2)
    r = tk // tq
    vd = DIFF_V_DIM
    m_ref[...] = jnp.full_like(m_ref, -jnp.inf)
    acc_ref[...] = jnp.zeros_like(acc_ref)
    ones = jnp.ones((tk, vd), BF16)

    q = q_ref[...]
    lane = lax.broadcasted_iota(jnp.int32, q.shape, 1)
    zero = jnp.zeros_like(q)
    halves = (jnp.where(lane < DIFF_HEAD_DIM, q, zero), jnp.where(lane >= DIFF_HEAD_DIM, q, zero))

    def chunk(j, carry):
        rows = pl.ds(pl.multiple_of(j * tk, tk), tk)
        k = k_ref[rows, :]
        vx = jnp.concatenate([v_ref[rows, :], ones], axis=1)
        bias = bias_ref[jnp.clip(j * r - i, -r - 1, r) + r + 1]
        for c, qc in enumerate(halves):
            s = lax.dot_general(qc, k, (((1,), (1,)), ((), ())), preferred_element_type=F32) + bias
            m_prev = m_ref[c]
            m_new = jnp.maximum(m_prev, jnp.max(s, axis=-1, keepdims=True))
            alpha = jnp.exp2(m_prev - m_new)
            p = jnp.exp2(s - m_new).astype(BF16)
            acc_ref[c] = alpha * acc_ref[c] + jnp.dot(p, vx, preferred_element_type=F32)
            m_ref[c] = m_new
        return carry

    lax.fori_loop(0, nkc, chunk, 0)

    lp = lam_ref[...]
    lam = (jnp.exp(jnp.sum(lp[0:1] * lp[1:2], axis=-1, keepdims=True))
           - jnp.exp(jnp.sum(lp[2:3] * lp[3:4], axis=-1, keepdims=True)) + lam_init)
    a0 = acc_ref[0]
    a1 = acc_ref[1]
    o = a0[:, :vd] / a0[:, vd:] - lam * (a1[:, :vd] / a1[:, vd:])
    o_ref[...] = (_rms(o, hg_ref[...]) * (1.0 - lam_init)).astype(o_ref.dtype)


def diff_attention(z, bias, lam_p, head_g, out, row0, nb, S, lam_init):
    tq, tk = bias.shape[2], bias.shape[3]
    assert S % tk == 0 and row0 % S == 0
    nq = S // tq
    rb = row0 // tq
    sb = row0 // S
    vd = DIFF_V_DIM
    q0 = FNET_WIDTH // vd
    k0 = (FNET_WIDTH + DIFF_QK) // vd
    v0 = (FNET_WIDTH + 2 * DIFF_QK) // vd
    o0 = FNET_WIDTH // vd
    return _call_into(
        functools.partial(_diff_attn_kernel, nkc=S // tk, tq=tq, tk=tk, lam_init=lam_init), out,
        (z, z, z, bias, lam_p.astype(F32), head_g.reshape(1, vd).astype(F32)),
        in_specs=[
            pl.BlockSpec((tq, vd), lambda b, h, i: (rb + b * nq + i, q0 + h)),
            pl.BlockSpec((S, vd), lambda b, h, i: (sb + b, k0 + h)),
            pl.BlockSpec((S, vd), lambda b, h, i: (sb + b, v0 + h)),
            pl.BlockSpec((bias.shape[0], None, tq, tk), lambda b, h, i: (0, h, 0, 0)),
            pl.BlockSpec((4, DIFF_HEAD_DIM), lambda b, h, i: (0, 0)),
            pl.BlockSpec((1, vd), lambda b, h, i: (0, 0)),
        ],
        grid=(nb, DIFF_HEADS, nq),
        out_specs=pl.BlockSpec((tq, vd), lambda b, h, i: (rb + b * nq + i, o0 + h)),
        out_shape=jax.ShapeDtypeStruct(out.shape, out.dtype),
        scratch_shapes=[
            pltpu.VMEM((2, tq, 1), F32),
            pltpu.VMEM((2, tq, 2 * vd), F32),
        ],
        compiler_params=_cparams("parallel", "parallel", "arbitrary"),
        name="diff_attn",
    )


def _log_sigmoid(x):
    return jnp.minimum(x, 0.0) - jnp.log(1.0 + jnp.exp(-jnp.abs(x)))


def _gla_chunk(q, k, v, pre, st, rev):
    C = GLA_CHUNK
    nt = (((1,), (1,)), ((), ()))
    g = _log_sigmoid(pre) * (1.0 / GLA_TAU)
    row = lax.broadcasted_iota(jnp.int32, (C, C), 0)
    col = lax.broadcasted_iota(jnp.int32, (C, C), 1)
    tri = jnp.where((row <= col) if rev else (row >= col), 1.0, 0.0).astype(BF16)
    g_hi = g.astype(BF16)
    g_lo = (g - g_hi.astype(F32)).astype(BF16)
    b = (jnp.dot(tri, g_hi, preferred_element_type=F32)
         + jnp.dot(tri, g_lo, preferred_element_type=F32))
    edge = 0 if rev else C - 1
    b_edge = b[edge:edge + 1, :]

    qe = (q * jnp.exp(b)).astype(BF16)
    inter = lax.dot_general(qe, st.astype(BF16), nt, preferred_element_type=F32)

    sub = GLA_SUB
    lane = lax.broadcasted_iota(jnp.int32, (sub, C), 1)
    trow = lax.broadcasted_iota(jnp.int32, (sub, 1), 0)
    pieces = []
    for j in range(C // sub):
        lo = j * sub
        hi = lo + sub
        qj = q[lo:hi]
        bj = b[lo:hi]
        aj = jnp.zeros((sub, C), F32)
        for s in range(sub):
            ks = k[lo + s:lo + s + 1]
            bs = b[lo + s:lo + s + 1]
            seen = (trow <= s) if rev else (trow >= s)
            e = jnp.exp(jnp.where(seen, bj - bs, -jnp.inf))
            colv = jnp.sum(qj * ks * e, axis=-1, keepdims=True)
            aj = aj + jnp.where(lane == lo + s, colv, 0.0)
        if (hi < C) if rev else (j > 0):
            mj = b[hi:hi + 1] if rev else b[lo - 1:lo]
            qt = (qj * jnp.exp(bj - mj)).astype(BF16)
            kt = (k * jnp.exp(jnp.minimum(mj - b, 0.0))).astype(BF16)
            off = lax.dot_general(qt, kt, nt, preferred_element_type=F32)
            aj = aj + jnp.where((lane >= hi) if rev else (lane < lo), off, 0.0)
        pieces.append(aj)
    attn = jnp.concatenate(pieces, axis=0)
    intra = jnp.dot(attn.astype(BF16), v, preferred_element_type=F32)

    kd = (k * jnp.exp(b_edge - b)).astype(BF16)
    vt = v.astype(F32).T.astype(BF16)
    st_new = st * jnp.exp(b_edge) + jnp.dot(vt, kd, preferred_element_type=F32)
    return inter + intra, st_new


def _gla_bidir_kernel(qf_ref, kf_ref, vf_ref, lrf_ref, qr_ref, kr_ref, vr_ref, lrr_ref,
                      wguf_ref, wgur_ref, bgf_ref, bgr_ref, of_ref, or_ref, stf_ref, str_ref,
                      *, rows, resets_f, resets_r):
    t = pl.program_id(1)

    def any_of(blocks):
        return functools.reduce(jnp.logical_or, [t == b for b in blocks])

    @pl.when(any_of(resets_f))
    def _():
        stf_ref[...] = jnp.zeros_like(stf_ref)

    @pl.when(any_of(resets_r))
    def _():
        str_ref[...] = jnp.zeros_like(str_ref)

    nc = rows // GLA_CHUNK
    sides = ((qf_ref, kf_ref, vf_ref, lrf_ref, wguf_ref[...], bgf_ref[...], of_ref, stf_ref, False),
             (qr_ref, kr_ref, vr_ref, lrr_ref, wgur_ref[...], bgr_ref[...], or_ref, str_ref, True))

    def step(c, carry):
        for q_ref, k_ref, v_ref, lr_ref, wgu, bg, o_ref, st_ref, rev in sides:
            cc = nc - 1 - c if rev else c
            sl = pl.ds(pl.multiple_of(cc * GLA_CHUNK, GLA_CHUNK), GLA_CHUNK)
            q = q_ref[sl, :].astype(F32) * (GLA_DK ** -0.5)
            k = k_ref[sl, :].astype(F32)
            v = v_ref[sl, :]
            pre = jnp.dot(lr_ref[sl, :].astype(BF16), wgu, preferred_element_type=F32) + bg
            out, st_new = _gla_chunk(q, k, v, pre, st_ref[...], rev)
            o_ref[sl, :] = out
            st_ref[...] = st_new
        return carry

    lax.fori_loop(0, nc, step, 0, unroll=True)


def gla_bidirectional(z, lr, wgu_f, wgu_r, bg_f, bg_r, segs):
    M = z.shape[0]
    rows = GLA_ROWS
    assert all(row0 % rows == 0 and S % rows == 0 for row0, _, S in segs)
    T = M // rows
    starts = [(row0 + b * S) // rows for row0, nb, S in segs for b in range(nb)]
    lasts = [(row0 + (b + 1) * S) // rows - 1 for row0, nb, S in segs for b in range(nb)]
    resets_f = tuple(starts)
    resets_r = tuple(T - 1 - b for b in lasts)
    kb = GLA_HK // GLA_DK
    vb = 2 * GLA_HK // GLA_DV

    def side(rblk):
        return [
            pl.BlockSpec((rows, GLA_DK), lambda h, t: (rblk(t), h)),
            pl.BlockSpec((rows, GLA_DK), lambda h, t: (rblk(t), kb + h)),
            pl.BlockSpec((rows, GLA_DV), lambda h, t: (rblk(t), vb + h)),
            pl.BlockSpec((rows, GATE_PAD), lambda h, t: (rblk(t), 0)),
        ]

    fwd = lambda t: t
    rev = lambda t: T - 1 - t
    wspec = pl.BlockSpec((GATE_PAD, GLA_DK), lambda h, t: (0, h))
    bspec = pl.BlockSpec((1, GLA_DK), lambda h, t: (0, h))
    return pl.pallas_call(
        functools.partial(_gla_bidir_kernel, rows=rows, resets_f=resets_f, resets_r=resets_r),
        grid=(GLA_HEADS, T),
        in_specs=side(fwd) + side(rev) + [wspec, wspec, bspec, bspec],
        out_specs=[pl.BlockSpec((rows, GLA_DV), lambda h, t: (fwd(t), h)),
                   pl.BlockSpec((rows, GLA_DV), lambda h, t: (rev(t), h))],
        out_shape=[jax.ShapeDtypeStruct((M, GLA_HV), F32), jax.ShapeDtypeStruct((M, GLA_HV), F32)],
        scratch_shapes=[pltpu.VMEM((GLA_DV, GLA_DK), F32), pltpu.VMEM((GLA_DV, GLA_DK), F32)],
        compiler_params=_cparams("parallel", "arbitrary"),
        name="gla_bidir",
    )(z, z, z, lr, z, z, z, lr, wgu_f, wgu_r, bg_f, bg_r)


def _gla_post_kernel(of_ref, ob_ref, r_ref, g_ref, o_ref):
    o = _rms(of_ref[...] + ob_ref[...], g_ref[...])
    r = r_ref[...].astype(F32)
    o_ref[...] = (o * (r / (1.0 + jnp.exp(-r)))).astype(o_ref.dtype)


def gla_post(o_f, o_b, z, head_g, tm=512):
    M = o_f.shape[0]
    tm = _pick(M, tm)
    r0 = (2 * GLA_HK + GLA_HV) // GLA_DV
    return pl.pallas_call(
        _gla_post_kernel,
        grid=(M // tm, GLA_HEADS),
        in_specs=[
            pl.BlockSpec((tm, GLA_DV), lambda i, h: (i, h)),
            pl.BlockSpec((tm, GLA_DV), lambda i, h: (i, h)),
            pl.BlockSpec((tm, GLA_DV), lambda i, h: (i, r0 + h)),
            pl.BlockSpec((1, GLA_DV), lambda i, h: (0, 0)),
        ],
        out_specs=pl.BlockSpec((tm, GLA_DV), lambda i, h: (i, h)),
        out_shape=jax.ShapeDtypeStruct((M, GLA_HV), BF16),
        compiler_params=_cparams("parallel", "parallel"),
        name="gla_post",
    )(o_f, o_b, z, head_g.reshape(1, GLA_DV).astype(F32))


def _cross_attn_kernel(q_ref, k_ref, v_ref, o_ref):
    for h in range(X_HEADS):
        cols = slice(h * X_HEAD_DIM, (h + 1) * X_HEAD_DIM)
        s = lax.dot_general(q_ref[:, cols], k_ref[:, cols], (((1,), (1,)), ((), ())),
                            preferred_element_type=F32) * (X_HEAD_DIM ** -0.5)
        m = jnp.max(s, axis=-1, keepdims=True)
        p = jnp.exp(s - m)
        p = p / jnp.sum(p, axis=-1, keepdims=True)
        o_ref[:, cols] = jnp.dot(p.astype(BF16), v_ref[:, cols],
                                 preferred_element_type=F32).astype(o_ref.dtype)


def cross_attention(q, kv, out, row0, nb, S, mem_b0, n_mem, tq=512):
    tq = _pick(S, tq)
    nq = S // tq
    rb = row0 // tq
    width = X_HEADS * X_HEAD_DIM
    return _call_into(
        _cross_attn_kernel, out, (q, kv, kv),
        in_specs=[
            pl.BlockSpec((tq, width), lambda b, i: (rb + b * nq + i, 0)),
            pl.BlockSpec((n_mem, width), lambda b, i: (mem_b0 + b, 0)),
            pl.BlockSpec((n_mem, width), lambda b, i: (mem_b0 + b, 1)),
        ],
        grid=(nb, nq),
        out_specs=pl.BlockSpec((tq, width), lambda b, i: (rb + b * nq + i, 0)),
        out_shape=jax.ShapeDtypeStruct(q.shape, q.dtype),
        compiler_params=_cparams("parallel", "parallel"),
        name="cross_attn",
    )


def _gelu_tanh(x):
    return 0.5 * x * (1.0 + jnp.tanh(math.sqrt(2.0 / math.pi) * (x + 0.044715 * x * x * x)))


def _up_conv_gate_kernel(x_ref, xp_ref, xn_ref, g_ref, wa_ref, wg_ref, cwa_ref, cwg_ref, cba_ref, cbg_ref,
                         o_ref, h_ref, *, tm, firsts, lasts):
    i = pl.program_id(0)
    H = BF16_SUBLANES

    def any_of(tiles):
        return functools.reduce(jnp.logical_or, [i == t for t in tiles])

    @pl.when(pl.program_id(1) == 0)
    def _():
        g = g_ref[...]
        h_ref[H:H + tm, :] = _rms(x_ref[...], g).astype(BF16)
        keep_prev = jnp.where(any_of(firsts), 0.0, 1.0)
        keep_next = jnp.where(any_of(lasts), 0.0, 1.0)
        h_ref[0:H, :] = (_rms(xp_ref[...], g) * keep_prev).astype(BF16)
        h_ref[H + tm:2 * H + tm, :] = (_rms(xn_ref[...], g) * keep_next).astype(BF16)

    h = h_ref[...]

    def conv(w_ref, cw_ref, cb_ref):
        u = jnp.dot(h, w_ref[...], preferred_element_type=F32)
        cw = cw_ref[...]
        return (cw[0:1] * u[H - 1:H - 1 + tm] + cw[1:2] * u[H:H + tm] + cw[2:3] * u[H + 1:H + 1 + tm]
                + cb_ref[...])

    a = conv(wa_ref, cwa_ref, cba_ref)
    gate = conv(wg_ref, cwg_ref, cbg_ref)
    o_ref[...] = (_gelu_tanh(gate) * a).astype(o_ref.dtype)


def up_conv_gate(x, g, w_up, conv_w, conv_b, segs, tm=1024, tn=512):
    M, K = x.shape
    F = w_up.shape[1] // 2
    H = BF16_SUBLANES
    tm = _pick(math.gcd(*[S for _, _, S in segs]), tm, H)
    tn = _pick(F, tn, 128)
    assert all(row0 % tm == 0 for row0, _, _ in segs)
    nj = F // tn
    hb = tm // H
    firsts = tuple((row0 + b * S) // tm for row0, nb, S in segs for b in range(nb))
    lasts = tuple((row0 + (b + 1) * S) // tm - 1 for row0, nb, S in segs for b in range(nb))
    return pl.pallas_call(
        functools.partial(_up_conv_gate_kernel, tm=tm, firsts=firsts, lasts=lasts),
        grid=(M // tm, nj),
        in_specs=[
            pl.BlockSpec((tm, K), lambda i, j: (i, 0)),
            pl.BlockSpec((H, K), lambda i, j: (jnp.maximum(i * hb - 1, 0), 0)),
            pl.BlockSpec((H, K), lambda i, j: (jnp.minimum((i + 1) * hb, M // H - 1), 0)),
            pl.BlockSpec((1, K), lambda i, j: (0, 0)),
            pl.BlockSpec((K, tn), lambda i, j: (0, j)),
            pl.BlockSpec((K, tn), lambda i, j: (0, nj + j)),
            pl.BlockSpec((3, tn), lambda i, j: (0, j)),
            pl.BlockSpec((3, tn), lambda i, j: (0, nj + j)),
            pl.BlockSpec((1, tn), lambda i, j: (0, j)),
            pl.BlockSpec((1, tn), lambda i, j: (0, nj + j)),
        ],
        out_specs=pl.BlockSpec((tm, tn), lambda i, j: (i, j)),
        out_shape=jax.ShapeDtypeStruct((M, F), BF16),
        scratch_shapes=[pltpu.VMEM((tm + 2 * H, K), BF16)],
        compiler_params=_cparams("parallel", "arbitrary"),
        name="up_conv_gate",
    )(x, x, x, g.reshape(1, K), w_up, w_up, conv_w, conv_w, conv_b.reshape(1, 2 * F), conv_b.reshape(1, 2 * F))


def kernel(x_prompt, x_sample, mem_prompt, mem_sample, norm_g, rel_bias, w_in_even, w_out_even,
           diff_lambda, diff_norm_g, w_in_odd, gla_gate_down, gla_gate_up, gla_gate_bias, gla_norm_g,
           w_out_odd, w_xq, w_xkv, w_xo, w_up, conv_w, conv_b, w_down):
    bp, sp, D = x_prompt.shape
    bs, ss, _ = x_sample.shape
    n_mem = mem_prompt.shape[1]
    segs = ((0, bp, sp), (bp * sp, bs, ss))
    M = bp * sp + bs * ss
    x = stack_rows(x_prompt.reshape(bp * sp, D), x_sample.reshape(bs * ss, D))
    mem = stack_rows(mem_prompt.reshape(bp * n_mem, D), mem_sample.reshape(bs * n_mem, D))
    mem_b0 = (0, bp)
    norm_g = norm_g.astype(F32)

    tk = min(DIFF_TK, sp, ss)
    bias = bias_tiles(rel_bias, min(DIFF_TQ, tk), tk)
    gd = FNET_GROUP_DIM
    c_ch, s_ch = _dft_tables(gd, gd ** -0.5)
    w_ch = jnp.concatenate([c_ch, -s_ch], axis=1).astype(BF16)
    fcat = {}
    for _, _, S in segs:
        if S not in fcat:
            c_s, s_s = _dft_tables(S, S ** -0.5)
            fcat[S] = jnp.concatenate([c_s, s_s], axis=1).astype(BF16)
    q_cols = (jnp.arange(w_in_even.shape[2]) >= FNET_WIDTH) & (jnp.arange(w_in_even.shape[2]) < FNET_WIDTH + DIFF_QK)
    col_scale = jnp.where(q_cols, DIFF_HEAD_DIM ** -0.5 * LOG2E, 1.0).astype(F32)

    for l in range(DEPTH):
        ng = norm_g[l]
        i = l // 2
        if l % 2 == 0:
            lam_init = 0.8 - 0.6 * math.exp(-0.3 * l)
            z = prenorm_matmul(x, ng[0], (w_in_even[i] * col_scale).astype(BF16), BF16)
            y = fnet_channel_dft(z, w_ch)
            mixed = None
            mixed_shape = jax.ShapeDtypeStruct((M, FNET_WIDTH + DIFF_WIDTH), BF16)
            for row0, nb, S in segs:
                mixed = fnet_seq_dft(fcat[S], y, mixed, mixed_shape, row0, nb, S)
            for row0, nb, S in segs:
                mixed = diff_attention(z, bias, diff_lambda[i], diff_norm_g[i], mixed, row0, nb, S, lam_init)
            x = matmul_postnorm_residual(mixed, w_out_even[i].astype(BF16), ng[1], x)
        else:
            z = prenorm_matmul(x, ng[0], w_in_odd[i].astype(BF16), BF16)
            w_gd = jnp.concatenate([gla_gate_down[i, 0], gla_gate_down[i, 1]], axis=1)
            w_gd = jnp.pad(w_gd, ((0, 0), (0, GATE_PAD - 2 * GLA_RANK))).astype(BF16)
            lr = prenorm_matmul(x, ng[0], w_gd, F32)
            wgu, bg = [], []
            for d in range(2):
                pad = ((d * GLA_RANK, GATE_PAD - (d + 1) * GLA_RANK), (0, 0))
                wgu.append(jnp.pad(gla_gate_up[i, d], pad).astype(BF16))
                bg.append(gla_gate_bias[i, d].reshape(1, GLA_HK).astype(F32))
            o_f, o_b = gla_bidirectional(z, lr, wgu[0], wgu[1], bg[0], bg[1], segs)
            mixed = gla_post(o_f, o_b, z, gla_norm_g[i])
            x = matmul_postnorm_residual(mixed, w_out_odd[i].astype(BF16), ng[1], x)

        q = prenorm_matmul(x, ng[2], w_xq[l].astype(BF16), BF16)
        kv = prenorm_matmul(mem, ng[4], w_xkv[l].astype(BF16), BF16)
        att = None
        for (row0, nb, S), b0 in zip(segs, mem_b0):
            att = cross_attention(q, kv, att, row0, nb, S, b0, n_mem)
        x = matmul_postnorm_residual(att, w_xo[l].astype(BF16), ng[3], x)

        act = up_conv_gate(x, ng[5], w_up[l].astype(BF16), conv_w[l].astype(F32), conv_b[l].astype(F32), segs)
        x = matmul_postnorm_residual(act, w_down[l].astype(BF16), ng[6], x)

    y_prompt = x[:bp * sp].reshape(bp, sp, D)
    y_sample = x[bp * sp:].reshape(bs, ss, D)
    return (y_prompt, y_sample)
```
